```python
import jax, jax.numpy as jnp
from jax import lax
import numpy as np

D_MODEL = 2048
BATCH = 4
SEQ = 2048
DEPTH = 2
DEC_BATCH = 128
DEC_SEQ = 8
PAST_LEN = 16384
PAGE_SIZE = 128

D_MIX = D_MODEL
D_POOL = D_MIX // 2
D_GATE = D_MIX - D_POOL
POOL_WINDOWS = (2, 4, 8, 16)
N_POOL_GROUPS = len(POOL_WINDOWS)
POOL_GROUP_DIM = D_POOL // N_POOL_GROUPS
POOL_BUF = max(POOL_WINDOWS) - 1
CHUNK = 128
N_GATE_HEADS = 8
GATE_HEAD_DIM = D_GATE // N_GATE_HEADS
D_IN = D_POOL + 2 * D_GATE
DFF_DENSE = ((8 * D_MODEL // 3 + 255) // 256) * 256
N_EXPERTS = 8
TOP_K = 2
DFF_EXPERT = 7 * D_MODEL // 2
N_DENSE = (DEPTH + 1) // 2
N_MOE = DEPTH // 2
EPS = 1e-6

kernel_name = "hybrid_pool_gmlp_decoder_step"


def rmsnorm(x, g):
    xf = x.astype(jnp.float32)
    y = xf * lax.rsqrt(jnp.mean(xf * xf, axis=-1, keepdims=True) + EPS)
    return (y * g.astype(jnp.float32)).astype(x.dtype)


def pool_mixer(p, prefix, start_pos, w_pool, pool_scale):
    b, l, _ = p.shape
    full = jnp.concatenate([prefix.astype(p.dtype), p], axis=1)
    cs = jnp.cumsum(full.astype(jnp.float32), axis=1)
    cs = jnp.concatenate([jnp.zeros((b, 1, D_POOL), jnp.float32), cs], axis=1)
    n_avail = start_pos + jnp.arange(l, dtype=jnp.int32) + 1
    means = []
    for gi, w in enumerate(POOL_WINDOWS):
        c0 = gi * POOL_GROUP_DIM
        c1 = c0 + POOL_GROUP_DIM
        hi = cs[:, POOL_BUF + 1:POOL_BUF + 1 + l, c0:c1]
        lo = cs[:, POOL_BUF + 1 - w:POOL_BUF + 1 - w + l, c0:c1]
        cnt = jnp.minimum(n_avail, w).astype(jnp.float32)[None, :, None]
        means.append((hi - lo) / cnt)
    r = jnp.concatenate(means, axis=-1) - p.astype(jnp.float32)
    r = r.astype(p.dtype).reshape(b, l, N_POOL_GROUPS, POOL_GROUP_DIM)
    out = jnp.einsum('blgc,gcd->blgd', r, w_pool).reshape(b, l, D_POOL) * pool_scale
    return out, full[:, -POOL_BUF:]


def spatial_gate(u, v, w_s, b_s, n_chunks, chunk_len):
    b = u.shape[0]
    vc = v.reshape(b, n_chunks, chunk_len, N_GATE_HEADS, GATE_HEAD_DIM)
    mask = jnp.tril(jnp.ones((CHUNK, CHUNK), w_s.dtype))
    ws = (w_s * mask)[:, :chunk_len, :chunk_len]
    bias = jnp.swapaxes(b_s[:, :chunk_len], 0, 1)[None, None, :, :, None]
    z = jnp.einsum('hts,bcshd->bcthd', ws, vc) + bias
    return u * z.reshape(u.shape)


def mixer_block(h, prefix, start_pos, n_chunks, chunk_len, g_mix, w_in, g_v, w_pool,
                pool_scale, w_s, b_s, w_out):
    xn = rmsnorm(h, g_mix)
    proj = jnp.einsum('bld,de->ble', xn, w_in)
    p = proj[..., :D_POOL]
    uv = jax.nn.gelu(proj[..., D_POOL:], approximate=False)
    u = uv[..., :D_GATE]
    v = rmsnorm(uv[..., D_GATE:], g_v)
    pool_out, buf = pool_mixer(p, prefix, start_pos, w_pool, pool_scale)
    gate_out = spatial_gate(u, v, w_s, b_s, n_chunks, chunk_len)
    y = jnp.einsum('ble,ed->bld', jnp.concatenate([pool_out, gate_out], axis=-1), w_out)
    return h + y, buf, v


def ffn_dense(h, g, wg, wu, wd):
    xn = rmsnorm(h, g)
    a = jax.nn.silu(xn @ wg) * (xn @ wu)
    return h + a @ wd


def ffn_moe(h, g, w_router, wg, wu, wd):
    b, l, d = h.shape
    xt = rmsnorm(h, g).reshape(b * l, d)
    logits = xt.astype(jnp.float32) @ w_router.astype(jnp.float32)
    vals, idx = lax.top_k(logits, TOP_K)
    gates = jax.nn.softmax(vals, axis=-1)
    combine = jnp.sum(jax.nn.one_hot(idx, N_EXPERTS, dtype=jnp.float32) * gates[..., None], axis=1)
    out = jnp.zeros((b * l, d), jnp.float32)
    for e in range(N_EXPERTS):
        a = jax.nn.silu(xt @ wg[e]) * (xt @ wu[e])
        out = out + combine[:, e:e + 1] * (a @ wd[e]).astype(jnp.float32)
    return h + out.astype(h.dtype).reshape(b, l, d)


def setup_inputs(seed: int = 0) -> dict:
    key = jax.random.key(seed)
    ks = jax.random.split(key, 24)

    def nrm(k, shape, scale):
        return jax.random.normal(k, shape, jnp.float32) * scale

    return {
        "x_prompt": nrm(ks[0], (BATCH, SEQ, D_MODEL), 1.0),
        "x_sample": nrm(ks[1], (DEC_BATCH, DEC_SEQ, D_MODEL), 1.0),
        "state_pool": nrm(ks[2], (DEPTH, DEC_BATCH, POOL_BUF, D_POOL), 1.0),
        "g_mix": 1.0 + nrm(ks[3], (DEPTH, D_MODEL), 0.02),
        "w_in": nrm(ks[4], (DEPTH, D_MODEL, D_IN), D_MODEL ** -0.5),
        "g_v": 1.0 + nrm(ks[5], (DEPTH, D_GATE), 0.02),
        "w_pool": nrm(ks[6], (DEPTH, N_POOL_GROUPS, POOL_GROUP_DIM, POOL_GROUP_DIM), POOL_GROUP_DIM ** -0.5),
        "pool_scale": 1.0 + nrm(ks[7], (DEPTH, D_POOL), 0.02),
        "w_s": nrm(ks[8], (DEPTH, N_GATE_HEADS, CHUNK, CHUNK), CHUNK ** -0.5),
        "b_s": 1.0 + nrm(ks[9], (DEPTH, N_GATE_HEADS, CHUNK), 0.02),
        "w_out": nrm(ks[10], (DEPTH, D_MIX, D_MODEL), D_MIX ** -0.5),
        "g_ffn": 1.0 + nrm(ks[11], (DEPTH, D_MODEL), 0.02),
        "dense_w_gate": nrm(ks[12], (N_DENSE, D_MODEL, DFF_DENSE), D_MODEL ** -0.5),
        "dense_w_up": nrm(ks[13], (N_DENSE, D_MODEL, DFF_DENSE), D_MODEL ** -0.5),
        "dense_w_down": nrm(ks[14], (N_DENSE, DFF_DENSE, D_MODEL), DFF_DENSE ** -0.5),
        "w_router": nrm(ks[15], (N_MOE, D_MODEL, N_EXPERTS), D_MODEL ** -0.5),
        "moe_w_gate": nrm(ks[16], (N_MOE, N_EXPERTS, D_MODEL, DFF_EXPERT), D_MODEL ** -0.5),
        "moe_w_up": nrm(ks[17], (N_MOE, N_EXPERTS, D_MODEL, DFF_EXPERT), D_MODEL ** -0.5),
        "moe_w_down": nrm(ks[18], (N_MOE, N_EXPERTS, DFF_EXPERT, D_MODEL), DFF_EXPERT ** -0.5),
        "g_final": 1.0 + nrm(ks[19], (D_MODEL,), 0.02),
    }


def reference(x_prompt, x_sample, state_pool, g_mix, w_in, g_v, w_pool, pool_scale, w_s, b_s,
              w_out, g_ffn, dense_w_gate, dense_w_up, dense_w_down, w_router, moe_w_gate,
              moe_w_up, moe_w_down, g_final):
    hp = x_prompt
    hs = x_sample
    n_chunks_p = hp.shape[1] // CHUNK
    zero_prefix = jnp.zeros((hp.shape[0], POOL_BUF, D_POOL), hp.dtype)
    bufs_p, bufs_s, vs_s = [], [], []
    for i in range(DEPTH):
        hp, bp, _ = mixer_block(hp, zero_prefix, 0, n_chunks_p, CHUNK, g_mix[i], w_in[i], g_v[i],
                                w_pool[i], pool_scale[i], w_s[i], b_s[i], w_out[i])
        hs, bsmp, v_s = mixer_block(hs, state_pool[i], PAST_LEN, 1, hs.shape[1], g_mix[i], w_in[i],
                                    g_v[i], w_pool[i], pool_scale[i], w_s[i], b_s[i], w_out[i])
        bufs_p.append(bp)
        bufs_s.append(bsmp)
        vs_s.append(v_s)
        j = i // 2
        if i % 2 == 0:
            hp = ffn_dense(hp, g_ffn[i], dense_w_gate[j], dense_w_up[j], dense_w_down[j])
            hs = ffn_dense(hs, g_ffn[i], dense_w_gate[j], dense_w_up[j], dense_w_down[j])
        else:
            hp = ffn_moe(hp, g_ffn[i], w_router[j], moe_w_gate[j], moe_w_up[j], moe_w_down[j])
            hs = ffn_moe(hs, g_ffn[i], w_router[j], moe_w_gate[j], moe_w_up[j], moe_w_down[j])
    y_prompt = rmsnorm(hp, g_final)
    y_sample = rmsnorm(hs, g_final)
    new_pool_prompt = jnp.stack(bufs_p)
    new_pool_sample = jnp.stack(bufs_s)
    new_v_sample = jnp.stack(vs_s)
    return (y_prompt, y_sample, new_pool_prompt, new_pool_sample, new_v_sample)
```

```python
import functools

import jax
import jax.numpy as jnp
from jax import lax
from jax.experimental import pallas as pl
from jax.experimental.pallas import tpu as pltpu

F32 = jnp.float32
BF16 = jnp.bfloat16
I32 = jnp.int32

EPS = 1e-6
POOL_WINDOWS = (2, 4, 8, 16)
POOL_BUF = max(POOL_WINDOWS) - 1
HALO = 16
CHUNK = 128
N_HEADS = 8
N_EXPERTS = 8
TOP_K = 2
PAST_LEN = 16384

V7X_VMEM_LIMIT = 56 * 1024 * 1024

ROW_TILE = 512
FF_TILE = 256
SUB_ROWS = 256
MOE_TILE = 1280
DENSE_TILE = 1024


def _params(sem):
    return pltpu.CompilerParams(dimension_semantics=sem, vmem_limit_bytes=V7X_VMEM_LIMIT)


def _rms(x, g):
    return x * lax.rsqrt(jnp.mean(x * x, axis=-1, keepdims=True) + EPS) * g


def _gelu(x):
    return 0.5 * x * (1.0 + lax.erf(x * (2.0 ** -0.5)))


def _inproj_body(*refs, two_addends):
    if two_addends:
        xa_ref, xb_ref, g_ref, w_ref, gv_ref, o_ref, wbf_ref = refs
    else:
        xa_ref, g_ref, w_ref, gv_ref, o_ref, wbf_ref = refs
    n = pl.program_id(0)

    @pl.when(pl.program_id(1) == 0)
    def _():
        wbf_ref[...] = w_ref[...].astype(BF16)

    x = xa_ref[...]
    if two_addends:
        x = x + xb_ref[...]
    xn = _rms(x, g_ref[...]).astype(BF16)
    acc = jnp.dot(xn, wbf_ref[...], preferred_element_type=F32)

    @pl.when(n == 0)
    def _():
        o_ref[...] = acc

    @pl.when(n == 1)
    def _():
        o_ref[...] = _gelu(acc)

    @pl.when(n == 2)
    def _():
        o_ref[...] = _rms(_gelu(acc), gv_ref[...])


def _inproj(layer, xs, g_mix, w_in, g_v):
    t, d = xs[0].shape
    dn = w_in.shape[2] // 3
    x_spec = pl.BlockSpec((ROW_TILE, d), lambda n, m: (m, 0))
    return pl.pallas_call(
        functools.partial(_inproj_body, two_addends=len(xs) == 2),
        grid=(3, t // ROW_TILE),
        in_specs=[x_spec] * len(xs) + [
            pl.BlockSpec((None, 1, d), lambda n, m: (layer, 0, 0)),
            pl.BlockSpec((None, d, dn), lambda n, m: (layer, 0, n)),
            pl.BlockSpec((None, 1, dn), lambda n, m: (layer, 0, 0)),
        ],
        out_specs=pl.BlockSpec((None, ROW_TILE, dn), lambda n, m: (n, m, 0)),
        out_shape=jax.ShapeDtypeStruct((3, t, dn), F32),
        scratch_shapes=[pltpu.VMEM((d, dn), BF16)],
        compiler_params=_params(("arbitrary", "arbitrary")),
        name=f"inproj{layer}",
    )(*xs, g_mix[:, None, :], w_in, g_v[:, None, :])


def _mixer_body(*refs, rows, blocks_per_seq, start_pos, sample):
    if sample:
        p_ref, st_ref, u_ref, v_ref, wm_ref, bias_ref, wp_ref, ps_ref, o_ref, full_ref = refs
    else:
        p_ref, halo_ref, u_ref, v_ref, wm_ref, bias_ref, wp_ref, ps_ref, o_ref, full_ref = refs
    d_pool = p_ref.shape[1]
    gd = d_pool // len(POOL_WINDOWS)
    hd = u_ref.shape[1] // N_HEADS
    blk = pl.program_id(0) % blocks_per_seq
    p = p_ref[...]

    if sample:
        seqs = rows // 8
        full_ref[:, 1:HALO, :] = st_ref[...]
        full_ref[:, HALO:HALO + 8, :] = p.reshape(seqs, 8, d_pool)
    else:
        full_ref[0:HALO, :] = jnp.where(blk == 0, 0.0, halo_ref[...])
        full_ref[HALO:HALO + rows, :] = p
    row_id = lax.broadcasted_iota(I32, (rows, 1), 0)
    pos = start_pos + (row_id % 8 if sample else blk * rows + row_id)
    for gi, w in enumerate(POOL_WINDOWS):
        c0 = gi * gd
        s = None
        for j in range(w):
            if sample:
                term = full_ref[:, HALO - j:HALO - j + 8, c0:c0 + gd].reshape(rows, gd)
            else:
                term = full_ref[HALO - j:HALO - j + rows, c0:c0 + gd]
            s = term if s is None else s + term
        cnt = jnp.minimum(pos + 1, w).astype(F32)
        r = s / cnt - p[:, c0:c0 + gd]
        y = jnp.dot(r.astype(BF16), wp_ref[gi].astype(BF16), preferred_element_type=F32)
        o_ref[:, c0:c0 + gd] = (y * ps_ref[:, c0:c0 + gd]).astype(BF16)

    row = lax.broadcasted_iota(I32, (CHUNK, CHUNK), 0)
    col = lax.broadcasted_iota(I32, (CHUNK, CHUNK), 1)
    for h in range(N_HEADS):
        c0 = h * hd
        wm = jnp.where(row >= col, wm_ref[h], 0.0).astype(BF16)
        for c in range(rows // CHUNK):
            r0 = c * CHUNK
            z = jnp.dot(wm, v_ref[r0:r0 + CHUNK, c0:c0 + hd].astype(BF16),
                        preferred_element_type=F32) + bias_ref[:, c0:c0 + hd]
            o_ref[r0:r0 + CHUNK, d_pool + c0:d_pool + c0 + hd] = (
                u_ref[r0:r0 + CHUNK, c0:c0 + hd] * z).astype(BF16)


def _mixer(layer, puv, row0, n_rows, seq_len, start_pos, wm, bias, w_pool, pool_scale, state=None):
    dn = puv.shape[2]
    sample = state is not None
    rows = ROW_TILE
    blocks_per_seq = 1 if sample else seq_len // rows
    b0 = row0 // rows
    blk_spec = lambda k: pl.BlockSpec((None, rows, dn), lambda i: (k, b0 + i, 0))
    if sample:
        hist_spec = pl.BlockSpec((None, rows // 8, POOL_BUF, dn), lambda i: (layer, i, 0, 0))
        hist = state
        scratch = pltpu.VMEM((rows // 8, HALO + 8, dn), F32)
    else:
        per = rows // HALO
        hist_spec = pl.BlockSpec((None, HALO, dn), lambda i: (0, jnp.maximum((b0 + i) * per - 1, 0), 0))
        hist = puv
        scratch = pltpu.VMEM((HALO + rows, dn), F32)
    return pl.pallas_call(
        functools.partial(_mixer_body, rows=rows, blocks_per_seq=blocks_per_seq,
                          start_pos=start_pos, sample=sample),
        grid=(n_rows // rows,),
        in_specs=[
            blk_spec(0), hist_spec, blk_spec(1), blk_spec(2),
            pl.BlockSpec((N_HEADS, CHUNK, CHUNK), lambda i: (0, 0, 0)),
            pl.BlockSpec((CHUNK, dn), lambda i: (0, 0)),
            pl.BlockSpec((None,) + w_pool.shape[1:], lambda i: (layer, 0, 0, 0)),
            pl.BlockSpec((None, 1, dn), lambda i: (layer, 0, 0)),
        ],
        out_specs=pl.BlockSpec((rows, 2 * dn), lambda i: (i, 0)),
        out_shape=jax.ShapeDtypeStruct((n_rows, 2 * dn), BF16),
        scratch_shapes=[scratch],
        compiler_params=_params(("arbitrary",)),
        name=f"mixer{layer}_{'sample' if sample else 'prompt'}",
    )(puv, hist, puv, puv, wm, bias, w_pool, pool_scale[:, None, :])


def _outproj_body(*refs, two_addends):
    if two_addends:
        a_ref, w_ref, ha_ref, hb_ref, o_ref, wbf_ref = refs
    else:
        a_ref, w_ref, ha_ref, o_ref, wbf_ref = refs

    @pl.when(pl.program_id(1) == 0)
    def _():
        wbf_ref[...] = w_ref[...].astype(BF16)

    h = ha_ref[...]
    if two_addends:
        h = h + hb_ref[...]
    o_ref[...] = h + jnp.dot(a_ref[...], wbf_ref[...], preferred_element_type=F32)


def _outproj(layer, mix, w_out, hs):
    t, k = mix.shape
    d = w_out.shape[2]
    dn = d // 2
    h_spec = pl.BlockSpec((ROW_TILE, dn), lambda n, m: (m, n))
    return pl.pallas_call(
        functools.partial(_outproj_body, two_addends=len(hs) == 2),
        grid=(2, t // ROW_TILE),
        in_specs=[
            pl.BlockSpec((ROW_TILE, k), lambda n, m: (m, 0)),
            pl.BlockSpec((None, k, dn), lambda n, m: (layer, 0, n)),
        ] + [h_spec] * len(hs),
        out_specs=h_spec,
        out_shape=jax.ShapeDtypeStruct((t, d), F32),
        scratch_shapes=[pltpu.VMEM((k, dn), BF16)],
        compiler_params=_params(("arbitrary", "arbitrary")),
        name=f"outproj{layer}",
    )(mix, w_out, *hs)


def _norm_body(h_ref, g_ref, o_ref):
    o_ref[...] = _rms(h_ref[...], g_ref[...]).astype(BF16)


def _norm(layer, h, g):
    t, d = h.shape
    return pl.pallas_call(
        _norm_body,
        grid=(t // ROW_TILE,),
        in_specs=[pl.BlockSpec((ROW_TILE, d), lambda i: (i, 0)),
                  pl.BlockSpec((None, 1, d), lambda i: (layer, 0, 0))],
        out_specs=pl.BlockSpec((ROW_TILE, d), lambda i: (i, 0)),
        out_shape=jax.ShapeDtypeStruct((t, d), BF16),
        compiler_params=_params(("arbitrary",)),
        name=f"norm{layer}",
    )(h, g[:, None, :])


def _router_body(h_ref, g_ref, wr_ref, xn_ref, rank_ref, gate_ref, cin_ref, cnt_ref, carry_ref):
    tb = h_ref.shape[0]

    @pl.when(pl.program_id(0) == 0)
    def _():
        carry_ref[...] = jnp.zeros_like(carry_ref)

    xn = _rms(h_ref[...], g_ref[...])
    xn_ref[...] = xn.astype(BF16)
    logits = lax.dot_general(wr_ref[...], xn, (((1,), (1,)), ((), ())),
                             precision=lax.Precision.HIGHEST, preferred_element_type=F32)
    eidx = lax.broadcasted_iota(I32, logits.shape, 0)
    m1 = jnp.max(logits, axis=0, keepdims=True)
    i1 = jnp.min(jnp.where(logits == m1, eidx, N_EXPERTS), axis=0, keepdims=True)
    sel1 = eidx == i1
    rest = jnp.where(sel1, -jnp.inf, logits)
    m2 = jnp.max(rest, axis=0, keepdims=True)
    i2 = jnp.min(jnp.where(rest == m2, eidx, N_EXPERTS), axis=0, keepdims=True)
    sel2 = eidx == i2
    e2 = jnp.exp(m2 - m1)
    g1 = 1.0 / (1.0 + e2)
    g2 = e2 / (1.0 + e2)
    gate_ref[...] = jnp.where(sel1, g1, jnp.where(sel2, g2, 0.0))
    routed = jnp.where(sel1, 1.0, jnp.where(sel2, 1.0, 0.0))
    before = lax.broadcasted_iota(I32, (tb, tb), 0) < lax.broadcasted_iota(I32, (tb, tb), 1)
    upper = jnp.where(before, 1.0, 0.0).astype(BF16)
    excl = jnp.dot(routed.astype(BF16), upper, preferred_element_type=F32)
    carry = carry_ref[...]
    rank = excl + carry[:, 0:1]
    rank_ref[...] = jnp.where(routed > 0.0, rank, -1.0).astype(I32)
    cin_ref[...] = carry.astype(I32)
    new_carry = carry + jnp.sum(routed, axis=1, keepdims=True)
    carry_ref[...] = new_carry
    cnt_ref[...] = new_carry.astype(I32)


def _router(layer, h, g, w_router_t):
    t, d = h.shape
    nb = t // ROW_TILE
    e = w_router_t.shape[0]
    return pl.pallas_call(
        _router_body,
        grid=(nb,),
        in_specs=[pl.BlockSpec((ROW_TILE, d), lambda i: (i, 0)),
                  pl.BlockSpec((None, 1, d), lambda i: (layer, 0, 0)),
                  pl.BlockSpec((e, d), lambda i: (0, 0))],
        out_specs=[pl.BlockSpec((ROW_TILE, d), lambda i: (i, 0)),
                   pl.BlockSpec((e, ROW_TILE), lambda i: (0, i)),
                   pl.BlockSpec((e, ROW_TILE), lambda i: (0, i)),
                   pl.BlockSpec((None, e, 128), lambda i: (i, 0, 0)),
                   pl.BlockSpec((e, 128), lambda i: (0, 0))],
        out_shape=[jax.ShapeDtypeStruct((t, d), BF16),
                   jax.ShapeDtypeStruct((e, t), I32),
                   jax.ShapeDtypeStruct((e, t), F32),
                   jax.ShapeDtypeStruct((nb, e, 128), I32),
                   jax.ShapeDtypeStruct((e, 128), I32)],
        scratch_shapes=[pltpu.VMEM((e, 128), F32)],
        compiler_params=_params(("arbitrary",)),
        name=f"router{layer}",
    )(h, g[:, None, :], w_router_t)


def _ffn_body(te_ref, ns_ref, x_ref, wg_ref, wu_ref, wd_ref, o_ref, wgb_ref, wub_ref, wdb_ref):
    s = pl.program_id(0)
    f = pl.program_id(1)
    nsub = ns_ref[s]
    n_all = x_ref.shape[0] // SUB_ROWS

    @pl.when(nsub > 0)
    def _():
        wgb_ref[...] = wg_ref[...].astype(BF16)
        wub_ref[...] = wu_ref[...].astype(BF16)
        wdb_ref[...] = wd_ref[...].astype(BF16)

    def sub(m, carry):
        r0 = pl.multiple_of(m * SUB_ROWS, SUB_ROWS)
        x = x_ref[pl.ds(r0, SUB_ROWS), :]
        g = jnp.dot(x, wgb_ref[...], preferred_element_type=F32)
        u = jnp.dot(x, wub_ref[...], preferred_element_type=F32)
        a = (g * jax.nn.sigmoid(g) * u).astype(BF16)
        y = jnp.dot(a, wdb_ref[...], preferred_element_type=F32)

        @pl.when(f == 0)
        def _():
            o_ref[pl.ds(r0, SUB_ROWS), :] = y

        @pl.when(f > 0)
        def _():
            o_ref[pl.ds(r0, SUB_ROWS), :] += y

        return carry

    lax.fori_loop(0, nsub, sub, 0)

    @pl.when(f == 0)
    def _():
        def clear(m, carry):
            r0 = pl.multiple_of(m * SUB_ROWS, SUB_ROWS)
            o_ref[pl.ds(r0, SUB_ROWS), :] = jnp.zeros((SUB_ROWS, o_ref.shape[1]), F32)
            return carry

        lax.fori_loop(nsub, n_all, clear, 0)


def _ffn(layer, x, tile_expert, tile_nsub, wg, wu, wd, tile_rows):
    rows, d = x.shape
    ff = wg.shape[3]
    nf = ff // FF_TILE
    n_tiles = rows // tile_rows

    def w_col(s, f, te, ns):
        return (layer, te[s], 0, jnp.where(ns[s] > 0, f, nf - 1))

    def w_row(s, f, te, ns):
        return (layer, te[s], jnp.where(ns[s] > 0, f, nf - 1), 0)

    return pl.pallas_call(
        _ffn_body,
        grid_spec=pltpu.PrefetchScalarGridSpec(
            num_scalar_prefetch=2,
            grid=(n_tiles, nf),
            in_specs=[pl.BlockSpec((tile_rows, d), lambda s, f, te, ns: (s, 0)),
                      pl.BlockSpec((None, None, d, FF_TILE), w_col),
                      pl.BlockSpec((None, None, d, FF_TILE), w_col),
                      pl.BlockSpec((None, None, FF_TILE, d), w_row)],
            out_specs=pl.BlockSpec((tile_rows, d), lambda s, f, te, ns: (s, 0)),
            scratch_shapes=[pltpu.VMEM((d, FF_TILE), BF16), pltpu.VMEM((d, FF_TILE), BF16),
                            pltpu.VMEM((FF_TILE, d), BF16)]),
        out_shape=jax.ShapeDtypeStruct((rows, d), F32),
        compiler_params=_params(("arbitrary", "arbitrary")),
        name=f"ffn{layer}",
    )(tile_expert, tile_nsub, x, wg, wu, wd)


def _gather_body(pj_ref, pb_ref, pe_ref, prel_ref, pfirst_ref, plive_ref, rank_ref, xn_ref, o_ref):
    q = pl.program_id(0)
    rows, tb = o_ref.shape[0], xn_ref.shape[0]

    @pl.when(pfirst_ref[q] == 1)
    def _():
        o_ref[...] = jnp.zeros_like(o_ref)

    @pl.when(plive_ref[q] == 1)
    def _():
        rank_row = rank_ref[pl.ds(pe_ref[q], 1), :]
        want = lax.broadcasted_iota(I32, (rows, tb), 0) + prel_ref[q]
        onehot = jnp.where(want == rank_row, 1.0, 0.0).astype(BF16)
        o_ref[...] += jnp.dot(onehot, xn_ref[...], preferred_element_type=F32).astype(BF16)


def _gather(xn, rank, plan, n_rows):
    t, d = xn.shape
    e = rank.shape[0]
    pj, pb, pe, prel, pfirst, plive = plan
    return pl.pallas_call(
        _gather_body,
        grid_spec=pltpu.PrefetchScalarGridSpec(
            num_scalar_prefetch=6,
            grid=(pj.shape[0],),
            in_specs=[pl.BlockSpec((e, ROW_TILE), lambda q, pj, pb, *_: (0, pb[q])),
                      pl.BlockSpec((ROW_TILE, d), lambda q, pj, pb, *_: (pb[q], 0))],
            out_specs=pl.BlockSpec((SUB_ROWS, d), lambda q, pj, *_: (pj[q], 0))),
        out_shape=jax.ShapeDtypeStruct((n_rows, d), BF16),
        compiler_params=_params(("arbitrary",)),
        name="moe_gather",
    )(pj, pb, pe, prel, pfirst, plive, rank, xn)


def _combine_body(pb_ref, py_ref, pe_ref, prel_ref, pfirst_ref, plive_ref, plast_ref,
                  rank_ref, gate_ref, y_ref, h_ref, gf_ref, o_ref, *, final_norm):
    q = pl.program_id(0)
    tb, yr = h_ref.shape[0], y_ref.shape[0]

    @pl.when(pfirst_ref[q] == 1)
    def _():
        o_ref[...] = h_ref[...]

    @pl.when(plive_ref[q] == 1)
    def _():
        want = lax.broadcasted_iota(I32, (tb, yr), 1) + prel_ref[q]
        onehot = jnp.where(want == rank_ref[...], 1.0, 0.0).astype(BF16)
        picked = jnp.dot(onehot, y_ref[...].astype(BF16), preferred_element_type=F32)
        o_ref[...] += gate_ref[...] * picked

    if final_norm:
        @pl.when(plast_ref[q] == 1)
        def _():
            o_ref[...] = _rms(o_ref[...], gf_ref[...])


def _combine(h, y, rank_col, gate_col, plan, g_final):
    t, d = h.shape
    pb, py, pe, prel, pfirst, plive, plast = plan
    col_spec = pl.BlockSpec((None, ROW_TILE, 1), lambda q, pb, py, pe, *_: (pe[q], pb[q], 0))
    return pl.pallas_call(
        functools.partial(_combine_body, final_norm=True),
        grid_spec=pltpu.PrefetchScalarGridSpec(
            num_scalar_prefetch=7,
            grid=(pb.shape[0],),
            in_specs=[col_spec, col_spec,
                      pl.BlockSpec((SUB_ROWS, d), lambda q, pb, py, *_: (py[q], 0)),
                      pl.BlockSpec((ROW_TILE, d), lambda q, pb, *_: (pb[q], 0)),
                      pl.BlockSpec((1, d), lambda q, *_: (0, 0))],
            out_specs=pl.BlockSpec((ROW_TILE, d), lambda q, pb, *_: (pb[q], 0))),
        out_shape=jax.ShapeDtypeStruct((t, d), F32),
        compiler_params=_params(("arbitrary",)),
        name="moe_combine",
    )(pb, py, pe, prel, pfirst, plive, plast, rank_col, gate_col, y, h, g_final[None, :])


def _owner(cum, q):
    return jnp.sum(cum[None, :] <= q[:, None], axis=1).astype(I32)


def _moe_plan(cin, cnt, n_tok):
    nb = n_tok // ROW_TILE
    n_slots = n_tok * TOP_K
    s_max = n_slots // MOE_TILE + N_EXPERTS
    per_tile = MOE_TILE // SUB_ROWS
    counts = cnt[:, 0]
    cblk = jnp.concatenate([cin[:, :, 0].T, counts[:, None]], axis=1)
    ntile = (counts + MOE_TILE - 1) // MOE_TILE
    tcum = jnp.cumsum(ntile)
    tstart = tcum - ntile
    total = tcum[-1]
    s_idx = jnp.arange(s_max, dtype=I32)
    tile_ok = s_idx < total
    tile_e = _owner(tcum, jnp.minimum(s_idx, total - 1))
    tile_k = s_idx - tstart[tile_e]
    tile_rows = jnp.where(tile_ok, jnp.clip(counts[tile_e] - tile_k * MOE_TILE, 0, MOE_TILE), 0)
    tile_nsub = ((tile_rows + SUB_ROWS - 1) // SUB_ROWS).astype(I32)
    region = (tstart * MOE_TILE).astype(I32)

    nj = s_max * per_tile
    j = jnp.arange(nj, dtype=I32)
    sj = j // per_tile
    ej = tile_e[sj]
    rel_j = tile_k[sj] * MOE_TILE + (j % per_tile) * SUB_ROWS
    ok_j = tile_ok[sj] & (rel_j < counts[ej])
    cb = cblk[ej]
    b_lo = jnp.sum(cb[:, 1:] <= rel_j[:, None], axis=1)
    b_hi = jnp.sum(cb[:, :-1] < (rel_j + SUB_ROWS)[:, None], axis=1) - 1
    b_lo = jnp.clip(b_lo, 0, nb - 1)
    b_hi = jnp.clip(jnp.maximum(b_hi, b_lo), 0, nb - 1)
    npair = jnp.where(ok_j, b_hi - b_lo + 1, 1)
    b_lo = jnp.where(ok_j, b_lo, 0)
    pcum = jnp.cumsum(npair)
    pstart = pcum - npair
    n_g = nj + nb * N_EXPERTS
    q = jnp.arange(n_g, dtype=I32)
    alive = q < pcum[-1]
    jq = _owner(pcum, jnp.minimum(q, pcum[-1] - 1))
    kq = jnp.minimum(q, pcum[-1] - 1) - pstart[jq]
    gplan = (jq, (b_lo[jq] + kq).astype(I32), ej[jq], rel_j[jq].astype(I32),
             (alive & (kq == 0)).astype(I32), (alive & ok_j[jq]).astype(I32))

    span = cblk[:, 1:] - cblk[:, :-1]
    start = region[:, None] + cblk[:, :-1]
    r_lo = start // SUB_ROWS
    r_hi = (start + span - 1) // SUB_ROWS
    ncomb = jnp.where(span > 0, r_hi - r_lo + 1, 0).T.reshape(-1)
    r_lo = r_lo.T.reshape(-1)
    ccum = jnp.cumsum(ncomb)
    cstart = ccum - ncomb
    n_c = nb * N_EXPERTS * (ROW_TILE // SUB_ROWS + 1)
    q = jnp.arange(n_c, dtype=I32)
    alive = q < ccum[-1]
    qc = jnp.minimum(q, ccum[-1] - 1)
    be = _owner(ccum, qc)
    kq = qc - cstart[be]
    pb = be // N_EXPERTS
    pe = be % N_EXPERTS
    py = (r_lo[be] + kq).astype(I32)
    prel = py * SUB_ROWS - region[pe]
    prev_b = jnp.concatenate([jnp.full((1,), -1, I32), pb[:-1]])
    next_b = jnp.concatenate([pb[1:], jnp.full((1,), -1, I32)])
    next_alive = jnp.concatenate([alive[1:], jnp.zeros((1,), bool)])
    first = alive & (pb != prev_b)
    last = alive & ((pb != next_b) | ~next_alive)
    cplan = (pb.astype(I32), py, pe.astype(I32), prel.astype(I32), first.astype(I32),
             alive.astype(I32), last.astype(I32))
    return tile_e, tile_nsub, gplan, cplan, s_max * MOE_TILE


def kernel(x_prompt, x_sample, state_pool, g_mix, w_in, g_v, w_pool, pool_scale, w_s, b_s, w_out,
           g_ffn, dense_w_gate, dense_w_up, dense_w_down, w_router, moe_w_gate, moe_w_up,
           moe_w_down, g_final):
    batch, seq, d = x_prompt.shape
    dec_batch, dec_seq, _ = x_sample.shape
    depth = w_in.shape[0]
    n_p = batch * seq
    n_s = dec_batch * dec_seq
    n_tok = n_p + n_s
    past_len = PAST_LEN
    assert dec_seq == 8 and depth == 2 and n_p % ROW_TILE == 0 and n_s % ROW_TILE == 0

    hs = (jnp.concatenate([x_prompt.reshape(n_p, d), x_sample.reshape(n_s, d)], axis=0),)
    dn = w_in.shape[2] // 3
    hd = dn // N_HEADS
    per_chunk = CHUNK // dec_seq
    pools_p, pools_s, vs_s = [], [], []
    out = None
    for i in range(depth):
        puv = _inproj(i, hs, g_mix, w_in, g_v)
        bias_p = jnp.repeat(b_s[i].T, hd, axis=1)
        ws_s = w_s[i][:, :dec_seq, :dec_seq]
        eye = jnp.eye(per_chunk, dtype=F32)
        wm_s = (eye[None, :, None, :, None] * ws_s[:, None, :, None, :]).reshape(N_HEADS, CHUNK, CHUNK)
        bias_s = jnp.tile(jnp.repeat(b_s[i][:, :dec_seq].T, hd, axis=1), (per_chunk, 1))
        mix_p = _mixer(i, puv, 0, n_p, seq, 0, w_s[i], bias_p, w_pool, pool_scale)
        mix_s = _mixer(i, puv, n_p, n_s, dec_seq, past_len, wm_s, bias_s, w_pool, pool_scale,
                       state=state_pool)
        h = _outproj(i, jnp.concatenate([mix_p, mix_s], axis=0), w_out, hs)

        p = puv[0]
        pools_p.append(p[:n_p].reshape(batch, seq, dn)[:, seq - POOL_BUF:])
        pools_s.append(jnp.concatenate(
            [state_pool[i][:, dec_seq:], p[n_p:].reshape(dec_batch, dec_seq, dn)], axis=1))
        vs_s.append(puv[2][n_p:].reshape(dec_batch, dec_seq, dn))

        j = i // 2
        if i % 2 == 0:
            xn = _norm(i, h, g_ffn)
            n_tiles = n_tok // DENSE_TILE
            y = _ffn(j, xn, jnp.zeros((n_tiles,), I32),
                     jnp.full((n_tiles,), DENSE_TILE // SUB_ROWS, I32),
                     dense_w_gate[:, None], dense_w_up[:, None], dense_w_down[:, None], DENSE_TILE)
            hs = (h, y)
        else:
            xn, rank, gate, cin, cnt = _router(i, h, g_ffn, w_router[j].T)
            tile_e, tile_nsub, gplan, cplan, n_rows = _moe_plan(cin, cnt, n_tok)
            xs = _gather(xn, rank, gplan, n_rows)
            y = _ffn(j, xs, tile_e, tile_nsub, moe_w_gate, moe_w_up, moe_w_down, MOE_TILE)
            out = _combine(h, y, rank[:, :, None], gate[:, :, None], cplan, g_final)
            hs = (out,)

    y_prompt = out[:n_p].reshape(batch, seq, d)
    y_sample = out[n_p:].reshape(dec_batch, dec_seq, d)
    return (y_prompt, y_sample, jnp.stack(pools_p), jnp.stack(pools_s), jnp.stack(vs_s))
```

```python
import functools

import jax
import jax.numpy as jnp
from jax import lax
from jax.experimental import pallas as pl
from jax.experimental.pallas import tpu as pltpu

F32 = jnp.float32
BF16 = jnp.bfloat16
I32 = jnp.int32

EPS = 1e-6
POOL_WINDOWS = (2, 4, 8, 16)
POOL_BUF = max(POOL_WINDOWS) - 1
HALO = 16
CHUNK = 128
N_HEADS = 8
N_EXPERTS = 8
TOP_K = 2
PAST_LEN = 16384

V7X_VMEM_LIMIT = 56 * 1024 * 1024
V7X_MXU_COLS = 256

ROW_TILE = 512
FF_TILE = 256
SUB_ROWS = 256
MOE_TILE = 2560
DENSE_TILE = 2304


def _params(sem):
    return pltpu.CompilerParams(dimension_semantics=sem, vmem_limit_bytes=V7X_VMEM_LIMIT)


def _rms(x, g):
    return x * lax.rsqrt(jnp.mean(x * x, axis=-1, keepdims=True) + EPS) * g


def _gelu(x):
    return 0.5 * x * (1.0 + lax.erf(x * (2.0 ** -0.5)))


def _stack_offsets(stack):
    offs, o = [], 0
    for a in stack:
        offs.append(o)
        o += a.shape[0] // ROW_TILE
    return tuple(offs), o


def _stack_specs(stack, cols, m_axis, col_fn=None):
    offs, _ = _stack_offsets(stack)
    specs = []
    for a, off in zip(stack, offs):
        nblk = a.shape[0] // ROW_TILE

        def imap(*idx, off=off, nblk=nblk):
            col = 0 if col_fn is None else col_fn(*idx)
            return (jnp.clip(idx[m_axis] - off, 0, nblk - 1), col)

        specs.append(pl.BlockSpec((ROW_TILE, cols), imap))
    return specs


def _stack_read(refs, offs, m):
    v = refs[0][...]
    for ref, off in zip(refs[1:], offs[1:]):
        v = jnp.where(m >= off, ref[...], v)
    return v


def _sum_read(refs, layout, m):
    total, k = None, 0
    for offs in layout:
        part = _stack_read(refs[k:k + len(offs)], offs, m)
        k += len(offs)
        total = part if total is None else total + part
    return total


def _inproj_body(*refs, layout, kind):
    n_x = sum(len(offs) for offs in layout)
    g_ref, w_ref, gv_ref, o_ref, wbf_ref = refs[n_x:]
    m = pl.program_id(0)

    @pl.when(m == 0)
    def _():
        wbf_ref[...] = w_ref[...].astype(BF16)

    xn = _rms(_sum_read(refs[:n_x], layout, m), g_ref[...]).astype(BF16)
    dn = o_ref.shape[1]
    ss = None
    for c0 in range(0, dn, V7X_MXU_COLS):
        acc = jnp.dot(xn, wbf_ref[:, c0:c0 + V7X_MXU_COLS], preferred_element_type=F32)
        if kind != "pool":
            acc = _gelu(acc)
        if kind == "v":
            part = jnp.sum(acc * acc, axis=1, keepdims=True)
            ss = part if ss is None else ss + part
        o_ref[:, c0:c0 + V7X_MXU_COLS] = acc
    if kind == "v":
        o_ref[...] = o_ref[...] * lax.rsqrt(ss / dn + EPS) * gv_ref[...]


def _inproj(layer, addends, g_mix, w_in, g_v):
    d = addends[0][0].shape[1]
    dn = w_in.shape[2] // 3
    layout = tuple(_stack_offsets(st)[0] for st in addends)
    n_blk = _stack_offsets(addends[0])[1]
    x_specs = [sp for st in addends for sp in _stack_specs(st, d, 0)]
    outs = []
    for k, kind in enumerate(("pool", "u", "v")):
        outs.append(pl.pallas_call(
            functools.partial(_inproj_body, layout=layout, kind=kind),
            grid=(n_blk,),
            in_specs=x_specs + [
                pl.BlockSpec((None, 1, d), lambda m: (layer, 0, 0)),
                pl.BlockSpec((None, d, dn), lambda m, k=k: (layer, 0, k), pipeline_mode=pl.Buffered(1)),
                pl.BlockSpec((None, 1, dn), lambda m: (layer, 0, 0)),
            ],
            out_specs=pl.BlockSpec((ROW_TILE, dn), lambda m: (m, 0)),
            out_shape=jax.ShapeDtypeStruct((n_blk * ROW_TILE, dn), F32),
            scratch_shapes=[pltpu.VMEM((d, dn), BF16)],
            compiler_params=_params(("arbitrary",)),
            name=f"inproj{layer}_{kind}",
        )(*[a for st in addends for a in st], g_mix[:, None, :], w_in, g_v[:, None, :]))
    return outs


def _mixer_body(*refs, rows, blocks_per_seq, start_pos, sample):
    p_ref, hist_ref, u_ref, v_ref, wm_ref, bias_ref, wp_ref, ps_ref, o_ref, full_ref = refs
    d_pool = p_ref.shape[1]
    gd = d_pool // len(POOL_WINDOWS)
    hd = u_ref.shape[1] // N_HEADS
    blk = pl.program_id(0) % blocks_per_seq
    p = p_ref[...]

    if sample:
        seqs = rows // 8
        full_ref[:, 1:HALO, :] = hist_ref[...]
        full_ref[:, HALO:HALO + 8, :] = p.reshape(seqs, 8, d_pool)
    else:
        full_ref[0:HALO, :] = jnp.where(blk == 0, 0.0, hist_ref[...])
        full_ref[HALO:HALO + rows, :] = p
    row_id = lax.broadcasted_iota(I32, (rows, 1), 0)
    pos = start_pos + (row_id % 8 if sample else blk * rows + row_id)
    for gi, w in enumerate(POOL_WINDOWS):
        c0 = gi * gd
        s = None
        for j in range(w):
            if sample:
                term = full_ref[:, HALO - j:HALO - j + 8, c0:c0 + gd].reshape(rows, gd)
            else:
                term = full_ref[HALO - j:HALO - j + rows, c0:c0 + gd]
            s = term if s is None else s + term
        cnt = jnp.minimum(pos + 1, w).astype(F32)
        r = s / cnt - p[:, c0:c0 + gd]
        y = jnp.dot(r.astype(BF16), wp_ref[gi].astype(BF16), preferred_element_type=F32)
        o_ref[:, c0:c0 + gd] = (y * ps_ref[:, c0:c0 + gd]).astype(BF16)

    row = lax.broadcasted_iota(I32, (CHUNK, CHUNK), 0)
    col = lax.broadcasted_iota(I32, (CHUNK, CHUNK), 1)
    for h in range(N_HEADS):
        c0 = h * hd
        wm = jnp.where(row >= col, wm_ref[h], 0.0).astype(BF16)
        for c in range(rows // CHUNK):
            r0 = c * CHUNK
            z = jnp.dot(wm, v_ref[r0:r0 + CHUNK, c0:c0 + hd].astype(BF16),
                        preferred_element_type=F32) + bias_ref[:, c0:c0 + hd]
            o_ref[r0:r0 + CHUNK, d_pool + c0:d_pool + c0 + hd] = (
                u_ref[r0:r0 + CHUNK, c0:c0 + hd] * z).astype(BF16)


def _mixer(layer, p, u, v, row0, n_rows, seq_len, start_pos, wm, bias, w_pool, pool_scale,
           state=None):
    dn = p.shape[1]
    sample = state is not None
    rows = ROW_TILE
    blocks_per_seq = 1 if sample else seq_len // rows
    b0 = row0 // rows
    blk_spec = pl.BlockSpec((rows, dn), lambda i: (b0 + i, 0))
    if sample:
        hist_spec = pl.BlockSpec((None, rows // 8, POOL_BUF, dn), lambda i: (layer, i, 0, 0))
        hist = state
        scratch = pltpu.VMEM((rows // 8, HALO + 8, dn), F32)
    else:
        per = rows // HALO
        hist_spec = pl.BlockSpec((HALO, dn), lambda i: (jnp.maximum((b0 + i) * per - 1, 0), 0))
        hist = p
        scratch = pltpu.VMEM((HALO + rows, dn), F32)
    return pl.pallas_call(
        functools.partial(_mixer_body, rows=rows, blocks_per_seq=blocks_per_seq,
                          start_pos=start_pos, sample=sample),
        grid=(n_rows // rows,),
        in_specs=[
            blk_spec, hist_spec, blk_spec, blk_spec,
            pl.BlockSpec((N_HEADS, CHUNK, CHUNK), lambda i: (0, 0, 0)),
            pl.BlockSpec((CHUNK, dn), lambda i: (0, 0)),
            pl.BlockSpec((None,) + w_pool.shape[1:], lambda i: (layer, 0, 0, 0)),
            pl.BlockSpec((None, 1, dn), lambda i: (layer, 0, 0)),
        ],
        out_specs=pl.BlockSpec((rows, 2 * dn), lambda i: (i, 0)),
        out_shape=jax.ShapeDtypeStruct((n_rows, 2 * dn), BF16),
        scratch_shapes=[scratch],
        compiler_params=_params(("arbitrary",)),
        name=f"mixer{layer}_{'sample' if sample else 'prompt'}",
    )(p, hist, u, v, wm, bias, w_pool, pool_scale[:, None, :])


def _outproj_body(*refs, mix_offs, layout):
    n_mix = len(mix_offs)
    n_h = sum(len(offs) for offs in layout)
    w_ref = refs[n_mix]
    h_refs = refs[n_mix + 1:n_mix + 1 + n_h]
    o_ref, wbf_ref = refs[n_mix + 1 + n_h:]
    m = pl.program_id(1)

    @pl.when(m == 0)
    def _():
        wbf_ref[...] = w_ref[...].astype(BF16)

    a = _stack_read(refs[:n_mix], mix_offs, m)
    o_ref[...] = _sum_read(h_refs, layout, m) + jnp.dot(a, wbf_ref[...], preferred_element_type=F32)


def _outproj(layer, mix, w_out, addends):
    k = mix[0].shape[1]
    d = w_out.shape[2]
    dn = d // 2
    mix_offs, n_blk = _stack_offsets(mix)
    layout = tuple(_stack_offsets(st)[0] for st in addends)
    h_specs = [sp for st in addends for sp in _stack_specs(st, dn, 1, col_fn=lambda n, m: n)]
    return pl.pallas_call(
        functools.partial(_outproj_body, mix_offs=mix_offs, layout=layout),
        grid=(2, n_blk),
        in_specs=_stack_specs(mix, k, 1) + [
            pl.BlockSpec((None, k, dn), lambda n, m: (layer, 0, n)),
        ] + h_specs,
        out_specs=pl.BlockSpec((ROW_TILE, dn), lambda n, m: (m, n)),
        out_shape=jax.ShapeDtypeStruct((n_blk * ROW_TILE, d), F32),
        scratch_shapes=[pltpu.VMEM((k, dn), BF16)],
        compiler_params=_params(("arbitrary", "arbitrary")),
        name=f"outproj{layer}",
    )(*mix, w_out, *[a for st in addends for a in st])


def _norm_body(h_ref, g_ref, o_ref):
    o_ref[...] = _rms(h_ref[...], g_ref[...]).astype(BF16)


def _norm(layer, h, g):
    t, d = h.shape
    return pl.pallas_call(
        _norm_body,
        grid=(t // ROW_TILE,),
        in_specs=[pl.BlockSpec((ROW_TILE, d), lambda i: (i, 0)),
                  pl.BlockSpec((None, 1, d), lambda i: (layer, 0, 0))],
        out_specs=pl.BlockSpec((ROW_TILE, d), lambda i: (i, 0)),
        out_shape=jax.ShapeDtypeStruct((t, d), BF16),
        compiler_params=_params(("arbitrary",)),
        name=f"norm{layer}",
    )(h, g[:, None, :])


def _router_body(h_ref, g_ref, wr_ref, xn_ref, rank_ref, col_ref, cin_ref, cnt_ref, carry_ref):
    tb = h_ref.shape[0]

    @pl.when(pl.program_id(0) == 0)
    def _():
        carry_ref[...] = jnp.zeros_like(carry_ref)

    xn = _rms(h_ref[...], g_ref[...])
    xn_ref[...] = xn.astype(BF16)
    logits = lax.dot_general(wr_ref[...], xn, (((1,), (1,)), ((), ())),
                             precision=lax.Precision.HIGHEST, preferred_element_type=F32)
    eidx = lax.broadcasted_iota(I32, logits.shape, 0)
    m1 = jnp.max(logits, axis=0, keepdims=True)
    i1 = jnp.min(jnp.where(logits == m1, eidx, N_EXPERTS), axis=0, keepdims=True)
    sel1 = eidx == i1
    rest = jnp.where(sel1, -jnp.inf, logits)
    m2 = jnp.max(rest, axis=0, keepdims=True)
    i2 = jnp.min(jnp.where(rest == m2, eidx, N_EXPERTS), axis=0, keepdims=True)
    sel2 = eidx == i2
    e2 = jnp.exp(m2 - m1)
    g1 = 1.0 / (1.0 + e2)
    g2 = e2 / (1.0 + e2)
    gate = jnp.where(sel1, g1, jnp.where(sel2, g2, 0.0))
    routed = jnp.where(sel1, 1.0, jnp.where(sel2, 1.0, 0.0))
    before = lax.broadcasted_iota(I32, (tb, tb), 0) < lax.broadcasted_iota(I32, (tb, tb), 1)
    upper = jnp.where(before, 1.0, 0.0).astype(BF16)
    excl = jnp.dot(routed.astype(BF16), upper, preferred_element_type=F32)
    carry = carry_ref[...]
    rank = jnp.where(routed > 0.0, excl + carry[:, 0:1], -1.0)
    rank_ref[...] = rank.astype(I32)
    same = lax.broadcasted_iota(I32, (tb, tb), 0) == lax.broadcasted_iota(I32, (tb, tb), 1)
    col_ref[...] = lax.dot_general(jnp.where(same, 1.0, 0.0), jnp.concatenate([rank, gate], axis=0),
                                   (((1,), (1,)), ((), ())),
                                   precision=lax.Precision.HIGHEST, preferred_element_type=F32)
    cin_ref[...] = carry.astype(I32)
    new_carry = carry + jnp.sum(routed, axis=1, keepdims=True)
    carry_ref[...] = new_carry
    cnt_ref[...] = new_carry.astype(I32)


def _router(layer, h, g, w_router_t):
    t, d = h.shape
    nb = t // ROW_TILE
    e = w_router_t.shape[0]
    return pl.pallas_call(
        _router_body,
        grid=(nb,),
        in_specs=[pl.BlockSpec((ROW_TILE, d), lambda i: (i, 0)),
                  pl.BlockSpec((None, 1, d), lambda i: (layer, 0, 0)),
                  pl.BlockSpec((e, d), lambda i: (0, 0))],
        out_specs=[pl.BlockSpec((ROW_TILE, d), lambda i: (i, 0)),
                   pl.BlockSpec((e, ROW_TILE), lambda i: (0, i)),
                   pl.BlockSpec((ROW_TILE, 2 * e), lambda i: (i, 0)),
                   pl.BlockSpec((None, e, 128), lambda i: (i, 0, 0)),
                   pl.BlockSpec((e, 128), lambda i: (0, 0))],
        out_shape=[jax.ShapeDtypeStruct((t, d), BF16),
                   jax.ShapeDtypeStruct((e, t), I32),
                   jax.ShapeDtypeStruct((t, 2 * e), F32),
                   jax.ShapeDtypeStruct((nb, e, 128), I32),
                   jax.ShapeDtypeStruct((e, 128), I32)],
        scratch_shapes=[pltpu.VMEM((e, 128), F32)],
        compiler_params=_params(("arbitrary",)),
        name=f"router{layer}",
    )(h, g[:, None, :], w_router_t)


def _ffn_body(te_ref, ns_ref, x_ref, wg_ref, wu_ref, wd_ref, o_ref, wgb_ref, wub_ref, wdb_ref):
    s = pl.program_id(0)
    f = pl.program_id(1)
    nsub = ns_ref[s]

    @pl.when(f == 0)
    def _():
        o_ref[...] = jnp.zeros_like(o_ref)

    def rows_block(r0, n_rows):
        x = x_ref[pl.ds(r0, n_rows), :]
        g = jnp.dot(x, wgb_ref[...], preferred_element_type=F32)
        u = jnp.dot(x, wub_ref[...], preferred_element_type=F32)
        a = (g * jax.nn.sigmoid(g) * u).astype(BF16)
        o_ref[pl.ds(r0, n_rows), :] += jnp.dot(a, wdb_ref[...], preferred_element_type=F32)

    @pl.when(nsub > 0)
    def _():
        wgb_ref[...] = wg_ref[...].astype(BF16)
        wub_ref[...] = wu_ref[...].astype(BF16)
        wdb_ref[...] = wd_ref[...].astype(BF16)

        def pair(m, carry):
            rows_block(pl.multiple_of(m * (2 * SUB_ROWS), 2 * SUB_ROWS), 2 * SUB_ROWS)
            return carry

        lax.fori_loop(0, nsub // 2, pair, 0)

        @pl.when(nsub % 2 == 1)
        def _():
            rows_block(pl.multiple_of((nsub - 1) * SUB_ROWS, SUB_ROWS), SUB_ROWS)


def _ffn(layer, x, tile_expert, tile_nsub, wg, wu, wd, tile_rows):
    rows, d = x.shape
    ff = wg.shape[3]
    nf = ff // FF_TILE
    n_tiles = rows // tile_rows

    def w_col(s, f, te, ns):
        return (layer, te[s], 0, jnp.where(ns[s] > 0, f, nf - 1))

    def w_row(s, f, te, ns):
        return (layer, te[s], jnp.where(ns[s] > 0, f, nf - 1), 0)

    return pl.pallas_call(
        _ffn_body,
        grid_spec=pltpu.PrefetchScalarGridSpec(
            num_scalar_prefetch=2,
            grid=(n_tiles, nf),
            in_specs=[pl.BlockSpec((tile_rows, d), lambda s, f, te, ns: (s, 0),
                                   pipeline_mode=pl.Buffered(1)),
                      pl.BlockSpec((None, None, d, FF_TILE), w_col),
                      pl.BlockSpec((None, None, d, FF_TILE), w_col),
                      pl.BlockSpec((None, None, FF_TILE, d), w_row)],
            out_specs=pl.BlockSpec((tile_rows, d), lambda s, f, te, ns: (s, 0),
                                   pipeline_mode=pl.Buffered(1)),
            scratch_shapes=[pltpu.VMEM((d, FF_TILE), BF16), pltpu.VMEM((d, FF_TILE), BF16),
                            pltpu.VMEM((FF_TILE, d), BF16)]),
        out_shape=jax.ShapeDtypeStruct((rows, d), F32),
        compiler_params=_params(("arbitrary", "arbitrary")),
        name=f"ffn{layer}",
    )(tile_expert, tile_nsub, x, wg, wu, wd)


def _gather_body(pj_ref, pb_ref, pe_ref, prel_ref, pfirst_ref, plive_ref, rank_ref, xn_ref, o_ref):
    q = pl.program_id(0)
    rows, tb = o_ref.shape[0], xn_ref.shape[0]

    @pl.when(pfirst_ref[q] == 1)
    def _():
        o_ref[...] = jnp.zeros_like(o_ref)

    @pl.when(plive_ref[q] == 1)
    def _():
        rank_row = rank_ref[pl.ds(pe_ref[q], 1), :]
        want = lax.broadcasted_iota(I32, (rows, tb), 0) + prel_ref[q]
        onehot = jnp.where(want == rank_row, 1.0, 0.0).astype(BF16)
        o_ref[...] += jnp.dot(onehot, xn_ref[...], preferred_element_type=F32).astype(BF16)


def _gather(xn, rank, plan, n_rows):
    t, d = xn.shape
    e = rank.shape[0]
    pj, pb, pe, prel, pfirst, plive = plan
    return pl.pallas_call(
        _gather_body,
        grid_spec=pltpu.PrefetchScalarGridSpec(
            num_scalar_prefetch=6,
            grid=(pj.shape[0],),
            in_specs=[pl.BlockSpec((e, ROW_TILE), lambda q, pj, pb, *_: (0, pb[q])),
                      pl.BlockSpec((ROW_TILE, d), lambda q, pj, pb, *_: (pb[q], 0))],
            out_specs=pl.BlockSpec((SUB_ROWS, d), lambda q, pj, *_: (pj[q], 0))),
        out_shape=jax.ShapeDtypeStruct((n_rows, d), BF16),
        compiler_params=_params(("arbitrary",)),
        name="moe_gather",
    )(pj, pb, pe, prel, pfirst, plive, rank, xn)


def _combine_body(pb_ref, py_ref, pe_ref, prel_ref, pfirst_ref, plive_ref, plast_ref,
                  col_ref, y_ref, h_ref, gf_ref, o_ref):
    q = pl.program_id(0)
    tb, yr = h_ref.shape[0], y_ref.shape[0]

    @pl.when(pfirst_ref[q] == 1)
    def _():
        o_ref[...] = h_ref[...]

    @pl.when(plive_ref[q] == 1)
    def _():
        e = pe_ref[q]
        cols = col_ref[...]
        lane = lax.broadcasted_iota(I32, cols.shape, 1)
        rank_col = jnp.sum(jnp.where(lane == e, cols, 0.0), axis=1, keepdims=True)
        gate_col = jnp.sum(jnp.where(lane == e + N_EXPERTS, cols, 0.0), axis=1, keepdims=True)
        want = (lax.broadcasted_iota(I32, (tb, yr), 1) + prel_ref[q]).astype(F32)
        onehot = jnp.where(want == rank_col, 1.0, 0.0).astype(BF16)
        picked = jnp.dot(onehot, y_ref[...].astype(BF16), preferred_element_type=F32)
        o_ref[...] += gate_col * picked

    @pl.when(plast_ref[q] == 1)
    def _():
        o_ref[...] = _rms(o_ref[...], gf_ref[...])


def _combine(h, y, cols, plan, g_final):
    t, d = h.shape
    pb, py, pe, prel, pfirst, plive, plast = plan
    return pl.pallas_call(
        _combine_body,
        grid_spec=pltpu.PrefetchScalarGridSpec(
            num_scalar_prefetch=7,
            grid=(pb.shape[0],),
            in_specs=[pl.BlockSpec((ROW_TILE, cols.shape[1]), lambda q, pb, *_: (pb[q], 0)),
                      pl.BlockSpec((SUB_ROWS, d), lambda q, pb, py, *_: (py[q], 0)),
                      pl.BlockSpec((ROW_TILE, d), lambda q, pb, *_: (pb[q], 0)),
                      pl.BlockSpec((1, d), lambda q, *_: (0, 0))],
            out_specs=pl.BlockSpec((ROW_TILE, d), lambda q, pb, *_: (pb[q], 0))),
        out_shape=jax.ShapeDtypeStruct((t, d), F32),
        compiler_params=_params(("arbitrary",)),
        name="moe_combine",
    )(pb, py, pe, prel, pfirst, plive, plast, cols, y, h, g_final[None, :])


def _owner(cum, q):
    return jnp.sum(cum[None, :] <= q[:, None], axis=1).astype(I32)


def _moe_plan(cin, cnt, n_tok):
    nb = n_tok // ROW_TILE
    n_slots = n_tok * TOP_K
    s_max = n_slots // MOE_TILE + N_EXPERTS
    per_tile = MOE_TILE // SUB_ROWS
    counts = cnt[:, 0]
    cblk = jnp.concatenate([cin[:, :, 0].T, counts[:, None]], axis=1)
    ntile = (counts + MOE_TILE - 1) // MOE_TILE
    tcum = jnp.cumsum(ntile)
    tstart = tcum - ntile
    total = tcum[-1]
    s_idx = jnp.arange(s_max, dtype=I32)
    tile_ok = s_idx < total
    tile_e = _owner(tcum, jnp.minimum(s_idx, total - 1))
    tile_k = s_idx - tstart[tile_e]
    tile_rows = jnp.where(tile_ok, jnp.clip(counts[tile_e] - tile_k * MOE_TILE, 0, MOE_TILE), 0)
    tile_nsub = ((tile_rows + SUB_ROWS - 1) // SUB_ROWS).astype(I32)
    region = (tstart * MOE_TILE).astype(I32)

    nj = s_max * per_tile
    j = jnp.arange(nj, dtype=I32)
    sj = j // per_tile
    ej = tile_e[sj]
    rel_j = tile_k[sj] * MOE_TILE + (j % per_tile) * SUB_ROWS
    ok_j = tile_ok[sj] & (rel_j < counts[ej])
    cb = cblk[ej]
    b_lo = jnp.sum(cb[:, 1:] <= rel_j[:, None], axis=1)
    b_hi = jnp.sum(cb[:, :-1] < (rel_j + SUB_ROWS)[:, None], axis=1) - 1
    b_lo = jnp.clip(b_lo, 0, nb - 1)
    b_hi = jnp.clip(jnp.maximum(b_hi, b_lo), 0, nb - 1)
    npair = jnp.where(ok_j, b_hi - b_lo + 1, 1)
    b_lo = jnp.where(ok_j, b_lo, 0)
    pcum = jnp.cumsum(npair)
    pstart = pcum - npair
    n_g = nj + nb * N_EXPERTS
    q = jnp.arange(n_g, dtype=I32)
    alive = q < pcum[-1]
    jq = _owner(pcum, jnp.minimum(q, pcum[-1] - 1))
    kq = jnp.minimum(q, pcum[-1] - 1) - pstart[jq]
    gplan = (jq, (b_lo[jq] + kq).astype(I32), ej[jq], rel_j[jq].astype(I32),
             (alive & (kq == 0)).astype(I32), (alive & ok_j[jq]).astype(I32))

    span = cblk[:, 1:] - cblk[:, :-1]
    start = region[:, None] + cblk[:, :-1]
    r_lo = start // SUB_ROWS
    r_hi = (start + span - 1) // SUB_ROWS
    ncomb = jnp.where(span > 0, r_hi - r_lo + 1, 0).T.reshape(-1)
    r_lo = r_lo.T.reshape(-1)
    ccum = jnp.cumsum(ncomb)
    cstart = ccum - ncomb
    n_c = nb * N_EXPERTS * (ROW_TILE // SUB_ROWS + 1)
    q = jnp.arange(n_c, dtype=I32)
    alive = q < ccum[-1]
    qc = jnp.minimum(q, ccum[-1] - 1)
    be = _owner(ccum, qc)
    kq = qc - cstart[be]
    pb = be // N_EXPERTS
    pe = be % N_EXPERTS
    py = (r_lo[be] + kq).astype(I32)
    prel = py * SUB_ROWS - region[pe]
    prev_b = jnp.concatenate([jnp.full((1,), -1, I32), pb[:-1]])
    next_b = jnp.concatenate([pb[1:], jnp.full((1,), -1, I32)])
    next_alive = jnp.concatenate([alive[1:], jnp.zeros((1,), bool)])
    first = alive & (pb != prev_b)
    last = alive & ((pb != next_b) | ~next_alive)
    cplan = (pb.astype(I32), py, pe.astype(I32), prel.astype(I32), first.astype(I32),
             alive.astype(I32), last.astype(I32))
    return tile_e, tile_nsub, gplan, cplan, s_max * MOE_TILE


def kernel(x_prompt, x_sample, state_pool, g_mix, w_in, g_v, w_pool, pool_scale, w_s, b_s, w_out,
           g_ffn, dense_w_gate, dense_w_up, dense_w_down, w_router, moe_w_gate, moe_w_up,
           moe_w_down, g_final):
    batch, seq, d = x_prompt.shape
    dec_batch, dec_seq, _ = x_sample.shape
    depth = w_in.shape[0]
    n_p = batch * seq
    n_s = dec_batch * dec_seq
    n_tok = n_p + n_s
    assert dec_seq == 8 and depth == 2 and n_p % ROW_TILE == 0 and n_s % ROW_TILE == 0

    addends = ((x_prompt.reshape(n_p, d), x_sample.reshape(n_s, d)),)
    dn = w_in.shape[2] // 3
    hd = dn // N_HEADS
    per_chunk = CHUNK // dec_seq
    pools_p, pools_s, vs_s = [], [], []
    out = None
    for i in range(depth):
        p, u, v = _inproj(i, addends, g_mix, w_in, g_v)
        bias_p = jnp.repeat(b_s[i].T, hd, axis=1)
        ws_s = w_s[i][:, :dec_seq, :dec_seq]
        eye = jnp.eye(per_chunk, dtype=F32)
        wm_s = (eye[None, :, None, :, None] * ws_s[:, None, :, None, :]).reshape(N_HEADS, CHUNK, CHUNK)
        bias_s = jnp.tile(jnp.repeat(b_s[i][:, :dec_seq].T, hd, axis=1), (per_chunk, 1))
        mix_p = _mixer(i, p, u, v, 0, n_p, seq, 0, w_s[i], bias_p, w_pool, pool_scale)
        mix_s = _mixer(i, p, u, v, n_p, n_s, dec_seq, PAST_LEN, wm_s, bias_s, w_pool, pool_scale,
                       state=state_pool)
        h = _outproj(i, (mix_p, mix_s), w_out, addends)

        pools_p.append(jnp.stack([p[(b + 1) * seq - POOL_BUF:(b + 1) * seq] for b in range(batch)]))
        pools_s.append(jnp.concatenate(
            [state_pool[i, :, dec_seq:], p[n_p:].reshape(dec_batch, dec_seq, dn)], axis=1))
        vs_s.append(v[n_p:].reshape(dec_batch, dec_seq, dn))

        j = i // 2
        if i % 2 == 0:
            xn = _norm(i, h, g_ffn)
            n_tiles = n_tok // DENSE_TILE
            y = _ffn(j, xn, jnp.zeros((n_tiles,), I32),
                     jnp.full((n_tiles,), DENSE_TILE // SUB_ROWS, I32),
                     dense_w_gate[:, None], dense_w_up[:, None], dense_w_down[:, None], DENSE_TILE)
            addends = ((h,), (y,))
        else:
            xn, rank, cols, cin, cnt = _router(i, h, g_ffn, w_router[j].T)
            tile_e, tile_nsub, gplan, cplan, n_rows = _moe_plan(cin, cnt, n_tok)
            xs = _gather(xn, rank, gplan, n_rows)
            y = _ffn(j, xs, tile_e, tile_nsub, moe_w_gate, moe_w_up, moe_w_down, MOE_TILE)
            out = _combine(h, y, cols, cplan, g_final)
            addends = ((out,),)

    y_prompt = out[:n_p].reshape(batch, seq, d)
    y_sample = out[n_p:].reshape(dec_batch, dec_seq, d)
    return (y_prompt, y_sample, jnp.stack(pools_p), jnp.stack(pools_s), jnp.stack(vs_s))
```

```python
import functools

import jax
import jax.numpy as jnp
from jax import lax
from jax.experimental import pallas as pl
from jax.experimental.pallas import tpu as pltpu

F32 = jnp.float32
BF16 = jnp.bfloat16
I32 = jnp.int32

EPS = 1e-6
POOL_WINDOWS = (2, 4, 8, 16)
POOL_BUF = max(POOL_WINDOWS) - 1
HALO = 16
CHUNK = 128
N_HEADS = 8
N_EXPERTS = 8
TOP_K = 2
PAST_LEN = 16384

V7X_VMEM_LIMIT = 56 * 1024 * 1024
V7X_MXU_COLS = 256

ROW_TILE = 512
FF_TILE = 256
SUB_ROWS = 256
MOE_TILE = 2560
DENSE_TILE = 2304


def _params(sem):
    return pltpu.CompilerParams(dimension_semantics=sem, vmem_limit_bytes=V7X_VMEM_LIMIT)


def _rms(x, g):
    return x * lax.rsqrt(jnp.mean(x * x, axis=-1, keepdims=True) + EPS) * g


def _gelu(x):
    return 0.5 * x * (1.0 + lax.erf(x * (2.0 ** -0.5)))


def _stack_offsets(stack):
    offs, o = [], 0
    for a in stack:
        offs.append(o)
        o += a.shape[0] // ROW_TILE
    return tuple(offs), o


def _stack_specs(stack, cols, m_axis, col_fn=None):
    offs, _ = _stack_offsets(stack)
    specs = []
    for a, off in zip(stack, offs):
        nblk = a.shape[0] // ROW_TILE

        def imap(*idx, off=off, nblk=nblk):
            col = 0 if col_fn is None else col_fn(*idx)
            return (jnp.clip(idx[m_axis] - off, 0, nblk - 1), col)

        specs.append(pl.BlockSpec((ROW_TILE, cols), imap))
    return specs


def _stack_read(refs, offs, m):
    v = refs[0][...]
    for ref, off in zip(refs[1:], offs[1:]):
        v = jnp.where(m >= off, ref[...], v)
    return v


def _sum_read(refs, layout, m):
    total, k = None, 0
    for offs in layout:
        part = _stack_read(refs[k:k + len(offs)], offs, m)
        k += len(offs)
        total = part if total is None else total + part
    return total


def _inproj_body(*refs, layout, kind):
    n_x = sum(len(offs) for offs in layout)
    g_ref, w_ref, gv_ref, o_ref, wbf_ref = refs[n_x:]
    m = pl.program_id(0)

    @pl.when(m == 0)
    def _():
        wbf_ref[...] = w_ref[...].astype(BF16)

    xn = _rms(_sum_read(refs[:n_x], layout, m), g_ref[...]).astype(BF16)
    dn = o_ref.shape[1]
    ss = None
    for c0 in range(0, dn, V7X_MXU_COLS):
        acc = jnp.dot(xn, wbf_ref[:, c0:c0 + V7X_MXU_COLS], preferred_element_type=F32)
        if kind != "pool":
            acc = _gelu(acc)
        if kind == "v":
            part = jnp.sum(acc * acc, axis=1, keepdims=True)
            ss = part if ss is None else ss + part
        o_ref[:, c0:c0 + V7X_MXU_COLS] = acc
    if kind == "v":
        o_ref[...] = o_ref[...] * lax.rsqrt(ss / dn + EPS) * gv_ref[...]


def _inproj(layer, addends, g_mix, w_in, g_v):
    d = addends[0][0].shape[1]
    dn = w_in.shape[2] // 3
    layout = tuple(_stack_offsets(st)[0] for st in addends)
    n_blk = _stack_offsets(addends[0])[1]
    x_specs = [sp for st in addends for sp in _stack_specs(st, d, 0)]
    outs = []
    for k, kind in enumerate(("pool", "u", "v")):
        outs.append(pl.pallas_call(
            functools.partial(_inproj_body, layout=layout, kind=kind),
            grid=(n_blk,),
            in_specs=x_specs + [
                pl.BlockSpec((None, 1, d), lambda m: (layer, 0, 0)),
                pl.BlockSpec((None, d, dn), lambda m, k=k: (layer, 0, k), pipeline_mode=pl.Buffered(1)),
                pl.BlockSpec((None, 1, dn), lambda m: (layer, 0, 0)),
            ],
            out_specs=pl.BlockSpec((ROW_TILE, dn), lambda m: (m, 0)),
            out_shape=jax.ShapeDtypeStruct((n_blk * ROW_TILE, dn), F32),
            scratch_shapes=[pltpu.VMEM((d, dn), BF16)],
            compiler_params=_params(("arbitrary",)),
            name=f"inproj{layer}_{kind}",
        )(*[a for st in addends for a in st], g_mix[:, None, :], w_in, g_v[:, None, :]))
    return outs


def _mixer_body(*refs, rows, blocks_per_seq, start_pos, sample):
    p_ref, hist_ref, u_ref, v_ref, wm_ref, bias_ref, wp_ref, ps_ref, o_ref, full_ref = refs
    d_pool = p_ref.shape[1]
    gd = d_pool // len(POOL_WINDOWS)
    hd = u_ref.shape[1] // N_HEADS
    blk = pl.program_id(0) % blocks_per_seq
    p = p_ref[...]

    if sample:
        seqs = rows // 8
        full_ref[:, 1:HALO, :] = hist_ref[...]
        full_ref[:, HALO:HALO + 8, :] = p.reshape(seqs, 8, d_pool)
    else:
        full_ref[0:HALO, :] = jnp.where(blk == 0, 0.0, hist_ref[...])
        full_ref[HALO:HALO + rows, :] = p
    row_id = lax.broadcasted_iota(I32, (rows, 1), 0)
    pos = start_pos + (row_id % 8 if sample else blk * rows + row_id)
    for gi, w in enumerate(POOL_WINDOWS):
        c0 = gi * gd
        s = None
        for j in range(w):
            if sample:
                term = full_ref[:, HALO - j:HALO - j + 8, c0:c0 + gd].reshape(rows, gd)
            else:
                term = full_ref[HALO - j:HALO - j + rows, c0:c0 + gd]
            s = term if s is None else s + term
        cnt = jnp.minimum(pos + 1, w).astype(F32)
        r = s / cnt - p[:, c0:c0 + gd]
        y = jnp.dot(r.astype(BF16), wp_ref[gi].astype(BF16), preferred_element_type=F32)
        o_ref[:, c0:c0 + gd] = (y * ps_ref[:, c0:c0 + gd]).astype(BF16)

    row = lax.broadcasted_iota(I32, (CHUNK, CHUNK), 0)
    col = lax.broadcasted_iota(I32, (CHUNK, CHUNK), 1)
    for h in range(N_HEADS):
        c0 = h * hd
        wm = jnp.where(row >= col, wm_ref[h], 0.0).astype(BF16)
        for c in range(rows // CHUNK):
            r0 = c * CHUNK
            z = jnp.dot(wm, v_ref[r0:r0 + CHUNK, c0:c0 + hd].astype(BF16),
                        preferred_element_type=F32) + bias_ref[:, c0:c0 + hd]
            o_ref[r0:r0 + CHUNK, d_pool + c0:d_pool + c0 + hd] = (
                u_ref[r0:r0 + CHUNK, c0:c0 + hd] * z).astype(BF16)


def _mixer(layer, p, u, v, row0, n_rows, seq_len, start_pos, wm, bias, w_pool, pool_scale,
           state=None):
    dn = p.shape[1]
    sample = state is not None
    rows = ROW_TILE
    blocks_per_seq = 1 if sample else seq_len // rows
    b0 = row0 // rows
    blk_spec = pl.BlockSpec((rows, dn), lambda i: (b0 + i, 0))
    if sample:
        hist_spec = pl.BlockSpec((None, rows // 8, POOL_BUF, dn), lambda i: (layer, i, 0, 0))
        hist = state
        scratch = pltpu.VMEM((rows // 8, HALO + 8, dn), F32)
    else:
        per = rows // HALO
        hist_spec = pl.BlockSpec((HALO, dn), lambda i: (jnp.maximum((b0 + i) * per - 1, 0), 0))
        hist = p
        scratch = pltpu.VMEM((HALO + rows, dn), F32)
    return pl.pallas_call(
        functools.partial(_mixer_body, rows=rows, blocks_per_seq=blocks_per_seq,
                          start_pos=start_pos, sample=sample),
        grid=(n_rows // rows,),
        in_specs=[
            blk_spec, hist_spec, blk_spec, blk_spec,
            pl.BlockSpec((N_HEADS, CHUNK, CHUNK), lambda i: (0, 0, 0)),
            pl.BlockSpec((CHUNK, dn), lambda i: (0, 0)),
            pl.BlockSpec((None,) + w_pool.shape[1:], lambda i: (layer, 0, 0, 0)),
            pl.BlockSpec((None, 1, dn), lambda i: (layer, 0, 0)),
        ],
        out_specs=pl.BlockSpec((rows, 2 * dn), lambda i: (i, 0)),
        out_shape=jax.ShapeDtypeStruct((n_rows, 2 * dn), BF16),
        scratch_shapes=[scratch],
        compiler_params=_params(("arbitrary",)),
        name=f"mixer{layer}_{'sample' if sample else 'prompt'}",
    )(p, hist, u, v, wm, bias, w_pool, pool_scale[:, None, :])


def _outproj_body(*refs, mix_offs, layout):
    n_mix = len(mix_offs)
    n_h = sum(len(offs) for offs in layout)
    w_ref = refs[n_mix]
    h_refs = refs[n_mix + 1:n_mix + 1 + n_h]
    o_ref, wbf_ref = refs[n_mix + 1 + n_h:]
    m = pl.program_id(1)

    @pl.when(m == 0)
    def _():
        wbf_ref[...] = w_ref[...].astype(BF16)

    a = _stack_read(refs[:n_mix], mix_offs, m)
    o_ref[...] = _sum_read(h_refs, layout, m) + jnp.dot(a, wbf_ref[...], preferred_element_type=F32)


def _outproj(layer, mix, w_out, addends):
    k = mix[0].shape[1]
    d = w_out.shape[2]
    dn = d // 2
    mix_offs, n_blk = _stack_offsets(mix)
    layout = tuple(_stack_offsets(st)[0] for st in addends)
    h_specs = [sp for st in addends for sp in _stack_specs(st, dn, 1, col_fn=lambda n, m: n)]
    return pl.pallas_call(
        functools.partial(_outproj_body, mix_offs=mix_offs, layout=layout),
        grid=(2, n_blk),
        in_specs=_stack_specs(mix, k, 1) + [
            pl.BlockSpec((None, k, dn), lambda n, m: (layer, 0, n)),
        ] + h_specs,
        out_specs=pl.BlockSpec((ROW_TILE, dn), lambda n, m: (m, n)),
        out_shape=jax.ShapeDtypeStruct((n_blk * ROW_TILE, d), F32),
        scratch_shapes=[pltpu.VMEM((k, dn), BF16)],
        compiler_params=_params(("arbitrary", "arbitrary")),
        name=f"outproj{layer}",
    )(*mix, w_out, *[a for st in addends for a in st])


def _norm_body(h_ref, g_ref, o_ref):
    o_ref[...] = _rms(h_ref[...], g_ref[...]).astype(BF16)


def _norm(layer, h, g):
    t, d = h.shape
    return pl.pallas_call(
        _norm_body,
        grid=(t // ROW_TILE,),
        in_specs=[pl.BlockSpec((ROW_TILE, d), lambda i: (i, 0)),
                  pl.BlockSpec((None, 1, d), lambda i: (layer, 0, 0))],
        out_specs=pl.BlockSpec((ROW_TILE, d), lambda i: (i, 0)),
        out_shape=jax.ShapeDtypeStruct((t, d), BF16),
        compiler_params=_params(("arbitrary",)),
        name=f"norm{layer}",
    )(h, g[:, None, :])


def _router_body(h_ref, g_ref, wr_ref, xn_ref, rank_ref, col_ref, cin_ref, cnt_ref, carry_ref):
    tb = h_ref.shape[0]

    @pl.when(pl.program_id(0) == 0)
    def _():
        carry_ref[...] = jnp.zeros_like(carry_ref)

    xn = _rms(h_ref[...], g_ref[...])
    xn_ref[...] = xn.astype(BF16)
    logits = lax.dot_general(wr_ref[...], xn, (((1,), (1,)), ((), ())),
                             precision=lax.Precision.HIGHEST, preferred_element_type=F32)
    eidx = lax.broadcasted_iota(I32, logits.shape, 0)
    m1 = jnp.max(logits, axis=0, keepdims=True)
    i1 = jnp.min(jnp.where(logits == m1, eidx, N_EXPERTS), axis=0, keepdims=True)
    sel1 = eidx == i1
    rest = jnp.where(sel1, -jnp.inf, logits)
    m2 = jnp.max(rest, axis=0, keepdims=True)
    i2 = jnp.min(jnp.where(rest == m2, eidx, N_EXPERTS), axis=0, keepdims=True)
    sel2 = eidx == i2
    e2 = jnp.exp(m2 - m1)
    g1 = 1.0 / (1.0 + e2)
    g2 = e2 / (1.0 + e2)
    gate = jnp.where(sel1, g1, jnp.where(sel2, g2, 0.0))
    routed = jnp.where(sel1, 1.0, jnp.where(sel2, 1.0, 0.0))
    before = lax.broadcasted_iota(I32, (tb, tb), 0) < lax.broadcasted_iota(I32, (tb, tb), 1)
    upper = jnp.where(before, 1.0, 0.0).astype(BF16)
    excl = jnp.dot(routed.astype(BF16), upper, preferred_element_type=F32)
    carry = carry_ref[...]
    rank = jnp.where(routed > 0.0, excl + carry[:, 0:1], -1.0)
    rank_ref[...] = rank.astype(I32)
    same = lax.broadcasted_iota(I32, (tb, tb), 0) == lax.broadcasted_iota(I32, (tb, tb), 1)
    col_ref[...] = lax.dot_general(jnp.where(same, 1.0, 0.0), jnp.concatenate([rank, gate], axis=0),
                                   (((1,), (1,)), ((), ())),
                                   precision=lax.Precision.HIGHEST, preferred_element_type=F32)
    cin_ref[...] = carry.astype(I32)
    new_carry = carry + jnp.sum(routed, axis=1, keepdims=True)
    carry_ref[...] = new_carry
    cnt_ref[...] = new_carry.astype(I32)


def _router(layer, h, g, w_router_t):
    t, d = h.shape
    nb = t // ROW_TILE
    e = w_router_t.shape[0]
    return pl.pallas_call(
        _router_body,
        grid=(nb,),
        in_specs=[pl.BlockSpec((ROW_TILE, d), lambda i: (i, 0)),
                  pl.BlockSpec((None, 1, d), lambda i: (layer, 0, 0)),
                  pl.BlockSpec((e, d), lambda i: (0, 0))],
        out_specs=[pl.BlockSpec((ROW_TILE, d), lambda i: (i, 0)),
                   pl.BlockSpec((e, ROW_TILE), lambda i: (0, i)),
                   pl.BlockSpec((ROW_TILE, 2 * e), lambda i: (i, 0)),
                   pl.BlockSpec((None, e, 128), lambda i: (i, 0, 0)),
                   pl.BlockSpec((e, 128), lambda i: (0, 0))],
        out_shape=[jax.ShapeDtypeStruct((t, d), BF16),
                   jax.ShapeDtypeStruct((e, t), I32),
                   jax.ShapeDtypeStruct((t, 2 * e), F32),
                   jax.ShapeDtypeStruct((nb, e, 128), I32),
                   jax.ShapeDtypeStruct((e, 128), I32)],
        scratch_shapes=[pltpu.VMEM((e, 128), F32)],
        compiler_params=_params(("arbitrary",)),
        name=f"router{layer}",
    )(h, g[:, None, :], w_router_t)


def _ffn_body(te_ref, ns_ref, x_ref, wg_ref, wu_ref, wd_ref, o_ref, *scratch):
    s = pl.program_id(0)
    f = pl.program_id(1)
    nsub = ns_ref[s]
    acc_ref = scratch[0] if scratch else o_ref

    @pl.when(f == 0)
    def _():
        acc_ref[...] = jnp.zeros_like(acc_ref)

    def rows_block(first_sub, n_sub):
        n_rows = n_sub * SUB_ROWS
        r0 = pl.multiple_of(first_sub * SUB_ROWS, SUB_ROWS)
        x = x_ref[pl.ds(r0, n_rows), :]
        g = jnp.dot(x, wg_ref[...].astype(BF16), preferred_element_type=F32)
        u = jnp.dot(x, wu_ref[...].astype(BF16), preferred_element_type=F32)
        a = (g * jax.nn.sigmoid(g) * u).astype(BF16)
        acc_ref[pl.ds(r0, n_rows), :] += jnp.dot(a, wd_ref[...].astype(BF16),
                                                 preferred_element_type=F32)

    def quad(m, carry):
        rows_block(m * 4, 4)
        return carry

    lax.fori_loop(0, nsub // 4, quad, 0)

    @pl.when(nsub % 4 >= 2)
    def _():
        rows_block((nsub // 4) * 4, 2)

    @pl.when(nsub % 2 == 1)
    def _():
        rows_block(nsub - 1, 1)

    if scratch:
        @pl.when(f == pl.num_programs(1) - 1)
        def _():
            o_ref[...] = acc_ref[...].astype(o_ref.dtype)


def _ffn(layer, x, tile_expert, tile_nsub, wg, wu, wd, tile_rows, out_dtype):
    rows, d = x.shape
    ff = wg.shape[3]
    nf = ff // FF_TILE
    n_tiles = rows // tile_rows

    def w_col(s, f, te, ns):
        return (layer, te[s], 0, jnp.where(ns[s] > 0, f, nf - 1))

    def w_row(s, f, te, ns):
        return (layer, te[s], jnp.where(ns[s] > 0, f, nf - 1), 0)

    return pl.pallas_call(
        _ffn_body,
        grid_spec=pltpu.PrefetchScalarGridSpec(
            num_scalar_prefetch=2,
            grid=(n_tiles, nf),
            in_specs=[pl.BlockSpec((tile_rows, d), lambda s, f, te, ns: (s, 0),
                                   pipeline_mode=pl.Buffered(1)),
                      pl.BlockSpec((None, None, d, FF_TILE), w_col),
                      pl.BlockSpec((None, None, d, FF_TILE), w_col),
                      pl.BlockSpec((None, None, FF_TILE, d), w_row)],
            out_specs=pl.BlockSpec((tile_rows, d), lambda s, f, te, ns: (s, 0),
                                   pipeline_mode=pl.Buffered(1)),
            scratch_shapes=[] if out_dtype == F32 else [pltpu.VMEM((tile_rows, d), F32)]),
        out_shape=jax.ShapeDtypeStruct((rows, d), out_dtype),
        compiler_params=_params(("arbitrary", "arbitrary")),
        name=f"ffn{layer}",
    )(tile_expert, tile_nsub, x, wg, wu, wd)


def _gather_body(pj_ref, pb_ref, pe_ref, prel_ref, pfirst_ref, plive_ref, rank_ref, xn_ref, o_ref):
    q = pl.program_id(0)
    rows, tb = o_ref.shape[0], xn_ref.shape[0]

    @pl.when(pfirst_ref[q] == 1)
    def _():
        o_ref[...] = jnp.zeros_like(o_ref)

    @pl.when(plive_ref[q] == 1)
    def _():
        rank_row = rank_ref[pl.ds(pe_ref[q], 1), :]
        want = lax.broadcasted_iota(I32, (rows, tb), 0) + prel_ref[q]
        onehot = jnp.where(want == rank_row, 1.0, 0.0).astype(BF16)
        o_ref[...] += jnp.dot(onehot, xn_ref[...], preferred_element_type=F32).astype(BF16)


def _gather(xn, rank, plan, n_rows):
    t, d = xn.shape
    e = rank.shape[0]
    pj, pb, pe, prel, pfirst, plive = plan
    return pl.pallas_call(
        _gather_body,
        grid_spec=pltpu.PrefetchScalarGridSpec(
            num_scalar_prefetch=6,
            grid=(pj.shape[0],),
            in_specs=[pl.BlockSpec((e, ROW_TILE), lambda q, pj, pb, *_: (0, pb[q])),
                      pl.BlockSpec((ROW_TILE, d), lambda q, pj, pb, *_: (pb[q], 0))],
            out_specs=pl.BlockSpec((SUB_ROWS, d), lambda q, pj, *_: (pj[q], 0))),
        out_shape=jax.ShapeDtypeStruct((n_rows, d), BF16),
        compiler_params=_params(("arbitrary",)),
        name="moe_gather",
    )(pj, pb, pe, prel, pfirst, plive, rank, xn)


def _combine_body(pb_ref, py_ref, pe_ref, prel_ref, pfirst_ref, plive_ref, plast_ref,
                  col_ref, y_ref, h_ref, gf_ref, op_ref, os_ref, acc_ref, *, prompt_blocks):
    q = pl.program_id(0)
    tb, yr = h_ref.shape[0], y_ref.shape[0]

    @pl.when(pfirst_ref[q] == 1)
    def _():
        acc_ref[...] = h_ref[...]

    @pl.when(plive_ref[q] == 1)
    def _():
        e = pe_ref[q]
        cols = col_ref[...]
        lane = lax.broadcasted_iota(I32, cols.shape, 1)
        rank_col = jnp.sum(jnp.where(lane == e, cols, 0.0), axis=1, keepdims=True)
        gate_col = jnp.sum(jnp.where(lane == e + N_EXPERTS, cols, 0.0), axis=1, keepdims=True)
        want = (lax.broadcasted_iota(I32, (tb, yr), 1) + prel_ref[q]).astype(F32)
        onehot = jnp.where(want == rank_col, 1.0, 0.0).astype(BF16)
        picked = jnp.dot(onehot, y_ref[...], preferred_element_type=F32)
        acc_ref[...] += gate_col * picked

    @pl.when(plast_ref[q] == 1)
    def _():
        res = _rms(acc_ref[...], gf_ref[...])

        @pl.when(pb_ref[q] < prompt_blocks)
        def _():
            op_ref[...] = res

        @pl.when(pb_ref[q] >= prompt_blocks)
        def _():
            os_ref[...] = res


def _combine(h, y, cols, plan, g_final, n_prompt):
    t, d = h.shape
    pb, py, pe, prel, pfirst, plive, plast = plan
    npb = n_prompt // ROW_TILE
    return pl.pallas_call(
        functools.partial(_combine_body, prompt_blocks=npb),
        grid_spec=pltpu.PrefetchScalarGridSpec(
            num_scalar_prefetch=7,
            grid=(pb.shape[0],),
            in_specs=[pl.BlockSpec((ROW_TILE, cols.shape[1]), lambda q, pb, *_: (pb[q], 0)),
                      pl.BlockSpec((SUB_ROWS, d), lambda q, pb, py, *_: (py[q], 0)),
                      pl.BlockSpec((ROW_TILE, d), lambda q, pb, *_: (pb[q], 0)),
                      pl.BlockSpec((1, d), lambda q, *_: (0, 0))],
            out_specs=[pl.BlockSpec((ROW_TILE, d), lambda q, pb, *_: (jnp.minimum(pb[q], npb - 1), 0)),
                       pl.BlockSpec((ROW_TILE, d), lambda q, pb, *_: (jnp.maximum(pb[q] - npb, 0), 0))],
            scratch_shapes=[pltpu.VMEM((ROW_TILE, d), F32)]),
        out_shape=[jax.ShapeDtypeStruct((n_prompt, d), F32),
                   jax.ShapeDtypeStruct((t - n_prompt, d), F32)],
        compiler_params=_params(("arbitrary",)),
        name="moe_combine",
    )(pb, py, pe, prel, pfirst, plive, plast, cols, y, h, g_final[None, :])


def _owner(cum, q):
    return jnp.sum(cum[None, :] <= q[:, None], axis=1).astype(I32)


def _moe_plan(cin, cnt, n_tok):
    nb = n_tok // ROW_TILE
    n_slots = n_tok * TOP_K
    s_max = n_slots // MOE_TILE + N_EXPERTS
    per_tile = MOE_TILE // SUB_ROWS
    counts = cnt[:, 0]
    cblk = jnp.concatenate([cin[:, :, 0].T, counts[:, None]], axis=1)
    ntile = (counts + MOE_TILE - 1) // MOE_TILE
    tcum = jnp.cumsum(ntile)
    tstart = tcum - ntile
    total = tcum[-1]
    s_idx = jnp.arange(s_max, dtype=I32)
    tile_ok = s_idx < total
    tile_e = _owner(tcum, jnp.minimum(s_idx, total - 1))
    tile_k = s_idx - tstart[tile_e]
    tile_rows = jnp.where(tile_ok, jnp.clip(counts[tile_e] - tile_k * MOE_TILE, 0, MOE_TILE), 0)
    tile_nsub = ((tile_rows + SUB_ROWS - 1) // SUB_ROWS).astype(I32)
    region = (tstart * MOE_TILE).astype(I32)

    nj = s_max * per_tile
    j = jnp.arange(nj, dtype=I32)
    sj = j // per_tile
    ej = tile_e[sj]
    rel_j = tile_k[sj] * MOE_TILE + (j % per_tile) * SUB_ROWS
    ok_j = tile_ok[sj] & (rel_j < counts[ej])
    cb = cblk[ej]
    b_lo = jnp.sum(cb[:, 1:] <= rel_j[:, None], axis=1)
    b_hi = jnp.sum(cb[:, :-1] < (rel_j + SUB_ROWS)[:, None], axis=1) - 1
    b_lo = jnp.clip(b_lo, 0, nb - 1)
    b_hi = jnp.clip(jnp.maximum(b_hi, b_lo), 0, nb - 1)
    npair = jnp.where(ok_j, b_hi - b_lo + 1, 1)
    b_lo = jnp.where(ok_j, b_lo, 0)
    pcum = jnp.cumsum(npair)
    pstart = pcum - npair
    n_g = nj + nb * N_EXPERTS
    q = jnp.arange(n_g, dtype=I32)
    alive = q < pcum[-1]
    jq = _owner(pcum, jnp.minimum(q, pcum[-1] - 1))
    kq = jnp.minimum(q, pcum[-1] - 1) - pstart[jq]
    gplan = (jq, (b_lo[jq] + kq).astype(I32), ej[jq], rel_j[jq].astype(I32),
             (alive & (kq == 0)).astype(I32), (alive & ok_j[jq]).astype(I32))

    span = cblk[:, 1:] - cblk[:, :-1]
    start = region[:, None] + cblk[:, :-1]
    r_lo = start // SUB_ROWS
    r_hi = (start + span - 1) // SUB_ROWS
    ncomb = jnp.where(span > 0, r_hi - r_lo + 1, 0).T.reshape(-1)
    r_lo = r_lo.T.reshape(-1)
    ccum = jnp.cumsum(ncomb)
    cstart = ccum - ncomb
    n_c = nb * N_EXPERTS * (ROW_TILE // SUB_ROWS + 1)
    q = jnp.arange(n_c, dtype=I32)
    alive = q < ccum[-1]
    qc = jnp.minimum(q, ccum[-1] - 1)
    be = _owner(ccum, qc)
    kq = qc - cstart[be]
    pb = be // N_EXPERTS
    pe = be % N_EXPERTS
    py = (r_lo[be] + kq).astype(I32)
    prel = py * SUB_ROWS - region[pe]
    prev_b = jnp.concatenate([jnp.full((1,), -1, I32), pb[:-1]])
    next_b = jnp.concatenate([pb[1:], jnp.full((1,), -1, I32)])
    next_alive = jnp.concatenate([alive[1:], jnp.zeros((1,), bool)])
    first = alive & (pb != prev_b)
    last = alive & ((pb != next_b) | ~next_alive)
    cplan = (pb.astype(I32), py, pe.astype(I32), prel.astype(I32), first.astype(I32),
             alive.astype(I32), last.astype(I32))
    return tile_e, tile_nsub, gplan, cplan, s_max * MOE_TILE


def kernel(x_prompt, x_sample, state_pool, g_mix, w_in, g_v, w_pool, pool_scale, w_s, b_s, w_out,
           g_ffn, dense_w_gate, dense_w_up, dense_w_down, w_router, moe_w_gate, moe_w_up,
           moe_w_down, g_final):
    batch, seq, d = x_prompt.shape
    dec_batch, dec_seq, _ = x_sample.shape
    depth = w_in.shape[0]
    n_p = batch * seq
    n_s = dec_batch * dec_seq
    n_tok = n_p + n_s
    assert dec_seq == 8 and depth == 2 and n_p % ROW_TILE == 0 and n_s % ROW_TILE == 0

    addends = ((x_prompt.reshape(n_p, d), x_sample.reshape(n_s, d)),)
    dn = w_in.shape[2] // 3
    hd = dn // N_HEADS
    per_chunk = CHUNK // dec_seq
    pools_p, pools_s, vs_s = [], [], []
    out = None
    for i in range(depth):
        p, u, v = _inproj(i, addends, g_mix, w_in, g_v)
        bias_p = jnp.repeat(b_s[i].T, hd, axis=1)
        ws_s = w_s[i][:, :dec_seq, :dec_seq]
        eye = jnp.eye(per_chunk, dtype=F32)
        wm_s = (eye[None, :, None, :, None] * ws_s[:, None, :, None, :]).reshape(N_HEADS, CHUNK, CHUNK)
        bias_s = jnp.tile(jnp.repeat(b_s[i][:, :dec_seq].T, hd, axis=1), (per_chunk, 1))
        mix_p = _mixer(i, p, u, v, 0, n_p, seq, 0, w_s[i], bias_p, w_pool, pool_scale)
        mix_s = _mixer(i, p, u, v, n_p, n_s, dec_seq, PAST_LEN, wm_s, bias_s, w_pool, pool_scale,
                       state=state_pool)
        h = _outproj(i, (mix_p, mix_s), w_out, addends)

        pools_p.append(jnp.stack([p[(b + 1) * seq - POOL_BUF:(b + 1) * seq] for b in range(batch)]))
        pools_s.append(jnp.concatenate(
            [state_pool[i, :, dec_seq:], p[n_p:].reshape(dec_batch, dec_seq, dn)], axis=1))
        vs_s.append(v[n_p:].reshape(dec_batch, dec_seq, dn))

        j = i // 2
        if i % 2 == 0:
            xn = _norm(i, h, g_ffn)
            n_tiles = n_tok // DENSE_TILE
            y = _ffn(j, xn, jnp.zeros((n_tiles,), I32),
                     jnp.full((n_tiles,), DENSE_TILE // SUB_ROWS, I32),
                     dense_w_gate[:, None], dense_w_up[:, None], dense_w_down[:, None], DENSE_TILE,
                     F32)
            addends = ((h,), (y,))
        else:
            xn, rank, cols, cin, cnt = _router(i, h, g_ffn, w_router[j].T)
            tile_e, tile_nsub, gplan, cplan, n_rows = _moe_plan(cin, cnt, n_tok)
            xs = _gather(xn, rank, gplan, n_rows)
            y = _ffn(j, xs, tile_e, tile_nsub, moe_w_gate, moe_w_up, moe_w_down, MOE_TILE, BF16)
            out = _combine(h, y, cols, cplan, g_final, n_p)
            addends = (tuple(out),)

    y_prompt = out[0].reshape(batch, seq, d)
    y_sample = out[1].reshape(dec_batch, dec_seq, d)
    return (y_prompt, y_sample, jnp.stack(pools_p), jnp.stack(pools_s), jnp.stack(vs_s))
```

```python
import functools

import jax
import jax.numpy as jnp
from jax import lax
from jax.experimental import pallas as pl
from jax.experimental.pallas import tpu as pltpu

F32 = jnp.float32
BF16 = jnp.bfloat16
I32 = jnp.int32

EPS = 1e-6
POOL_WINDOWS = (2, 4, 8, 16)
POOL_BUF = max(POOL_WINDOWS) - 1
HALO = 16
CHUNK = 128
N_HEADS = 8
N_EXPERTS = 8
TOP_K = 2
PAST_LEN = 16384

V7X_VMEM_LIMIT = 56 * 1024 * 1024
V7X_MXU_COLS = 256

ROW_TILE = 512
FF_TILE = 256
SUB_ROWS = 256
MOE_TILE = 2560
DENSE_TILE = 2304


def _params(sem):
    return pltpu.CompilerParams(dimension_semantics=sem, vmem_limit_bytes=V7X_VMEM_LIMIT)


def _rms(x, g):
    return x * lax.rsqrt(jnp.mean(x * x, axis=-1, keepdims=True) + EPS) * g


def _gelu(x):
    return 0.5 * x * (1.0 + lax.erf(x * (2.0 ** -0.5)))


def _stack_offsets(stack):
    offs, o = [], 0
    for a in stack:
        offs.append(o)
        o += a.shape[0] // ROW_TILE
    return tuple(offs), o


def _stack_specs(stack, cols, m_axis, col_fn=None):
    offs, _ = _stack_offsets(stack)
    specs = []
    for a, off in zip(stack, offs):
        nblk = a.shape[0] // ROW_TILE

        def imap(*idx, off=off, nblk=nblk):
            col = 0 if col_fn is None else col_fn(*idx)
            return (jnp.clip(idx[m_axis] - off, 0, nblk - 1), col)

        specs.append(pl.BlockSpec((ROW_TILE, cols), imap))
    return specs


def _stack_read(refs, offs, m):
    v = refs[0][...]
    for ref, off in zip(refs[1:], offs[1:]):
        v = jnp.where(m >= off, ref[...], v)
    return v


def _sum_read(refs, layout, m):
    total, k = None, 0
    for offs in layout:
        part = _stack_read(refs[k:k + len(offs)], offs, m)
        k += len(offs)
        total = part if total is None else total + part
    return total


def _inproj_body(*refs, layout, kind):
    n_x = sum(len(offs) for offs in layout)
    g_ref, w_ref, gv_ref, o_ref, wbf_ref = refs[n_x:]
    m = pl.program_id(0)

    @pl.when(m == 0)
    def _():
        wbf_ref[...] = w_ref[...].astype(BF16)

    xn = _rms(_sum_read(refs[:n_x], layout, m), g_ref[...]).astype(BF16)
    dn = o_ref.shape[1]
    ss = None
    for c0 in range(0, dn, V7X_MXU_COLS):
        acc = jnp.dot(xn, wbf_ref[:, c0:c0 + V7X_MXU_COLS], preferred_element_type=F32)
        if kind != "pool":
            acc = _gelu(acc)
        if kind == "v":
            part = jnp.sum(acc * acc, axis=1, keepdims=True)
            ss = part if ss is None else ss + part
        o_ref[:, c0:c0 + V7X_MXU_COLS] = acc
    if kind == "v":
        o_ref[...] = o_ref[...] * lax.rsqrt(ss / dn + EPS) * gv_ref[...]


def _inproj(layer, addends, g_mix, w_in, g_v):
    d = addends[0][0].shape[1]
    dn = w_in.shape[2] // 3
    layout = tuple(_stack_offsets(st)[0] for st in addends)
    n_blk = _stack_offsets(addends[0])[1]
    x_specs = [sp for st in addends for sp in _stack_specs(st, d, 0)]
    outs = []
    for k, kind in enumerate(("pool", "u", "v")):
        outs.append(pl.pallas_call(
            functools.partial(_inproj_body, layout=layout, kind=kind),
            grid=(n_blk,),
            in_specs=x_specs + [
                pl.BlockSpec((None, 1, d), lambda m: (layer, 0, 0)),
                pl.BlockSpec((None, d, dn), lambda m, k=k: (layer, 0, k), pipeline_mode=pl.Buffered(1)),
                pl.BlockSpec((None, 1, dn), lambda m: (layer, 0, 0)),
            ],
            out_specs=pl.BlockSpec((ROW_TILE, dn), lambda m: (m, 0)),
            out_shape=jax.ShapeDtypeStruct((n_blk * ROW_TILE, dn), F32),
            scratch_shapes=[pltpu.VMEM((d, dn), BF16)],
            compiler_params=_params(("arbitrary",)),
            name=f"inproj{layer}_{kind}",
        )(*[a for st in addends for a in st], g_mix[:, None, :], w_in, g_v[:, None, :]))
    return outs


def _mixer_body(*refs, rows, blocks_per_seq, start_pos, sample):
    p_ref, hist_ref, u_ref, v_ref, wm_ref, bias_ref, wp_ref, ps_ref, o_ref, full_ref = refs
    d_pool = p_ref.shape[1]
    gd = d_pool // len(POOL_WINDOWS)
    hd = u_ref.shape[1] // N_HEADS
    blk = pl.program_id(0) % blocks_per_seq
    p = p_ref[...]

    if sample:
        seqs = rows // 8
        full_ref[:, 1:HALO, :] = hist_ref[...]
        full_ref[:, HALO:HALO + 8, :] = p.reshape(seqs, 8, d_pool)
    else:
        full_ref[0:HALO, :] = jnp.where(blk == 0, 0.0, hist_ref[...])
        full_ref[HALO:HALO + rows, :] = p
    row_id = lax.broadcasted_iota(I32, (rows, 1), 0)
    pos = start_pos + (row_id % 8 if sample else blk * rows + row_id)
    for gi, w in enumerate(POOL_WINDOWS):
        c0 = gi * gd
        s = None
        for j in range(w):
            if sample:
                term = full_ref[:, HALO - j:HALO - j + 8, c0:c0 + gd].reshape(rows, gd)
            else:
                term = full_ref[HALO - j:HALO - j + rows, c0:c0 + gd]
            s = term if s is None else s + term
        cnt = jnp.minimum(pos + 1, w).astype(F32)
        r = s / cnt - p[:, c0:c0 + gd]
        y = jnp.dot(r.astype(BF16), wp_ref[gi].astype(BF16), preferred_element_type=F32)
        o_ref[:, c0:c0 + gd] = (y * ps_ref[:, c0:c0 + gd]).astype(BF16)

    row = lax.broadcasted_iota(I32, (CHUNK, CHUNK), 0)
    col = lax.broadcasted_iota(I32, (CHUNK, CHUNK), 1)
    for h in range(N_HEADS):
        c0 = h * hd
        wm = jnp.where(row >= col, wm_ref[h], 0.0).astype(BF16)
        for c in range(rows // CHUNK):
            r0 = c * CHUNK
            z = jnp.dot(wm, v_ref[r0:r0 + CHUNK, c0:c0 + hd].astype(BF16),
                        preferred_element_type=F32) + bias_ref[:, c0:c0 + hd]
            o_ref[r0:r0 + CHUNK, d_pool + c0:d_pool + c0 + hd] = (
                u_ref[r0:r0 + CHUNK, c0:c0 + hd] * z).astype(BF16)


def _mixer(layer, p, u, v, row0, n_rows, seq_len, start_pos, wm, bias, w_pool, pool_scale,
           state=None):
    dn = p.shape[1]
    sample = state is not None
    rows = ROW_TILE
    blocks_per_seq = 1 if sample else seq_len // rows
    b0 = row0 // rows
    blk_spec = pl.BlockSpec((rows, dn), lambda i: (b0 + i, 0))
    if sample:
        hist_spec = pl.BlockSpec((None, rows // 8, POOL_BUF, dn), lambda i: (layer, i, 0, 0))
        hist = state
        scratch = pltpu.VMEM((rows // 8, HALO + 8, dn), F32)
    else:
        per = rows // HALO
        hist_spec = pl.BlockSpec((HALO, dn), lambda i: (jnp.maximum((b0 + i) * per - 1, 0), 0))
        hist = p
        scratch = pltpu.VMEM((HALO + rows, dn), F32)
    return pl.pallas_call(
        functools.partial(_mixer_body, rows=rows, blocks_per_seq=blocks_per_seq,
                          start_pos=start_pos, sample=sample),
        grid=(n_rows // rows,),
        in_specs=[
            blk_spec, hist_spec, blk_spec, blk_spec,
            pl.BlockSpec((N_HEADS, CHUNK, CHUNK), lambda i: (0, 0, 0)),
            pl.BlockSpec((CHUNK, dn), lambda i: (0, 0)),
            pl.BlockSpec((None,) + w_pool.shape[1:], lambda i: (layer, 0, 0, 0)),
            pl.BlockSpec((None, 1, dn), lambda i: (layer, 0, 0)),
        ],
        out_specs=pl.BlockSpec((rows, 2 * dn), lambda i: (i, 0)),
        out_shape=jax.ShapeDtypeStruct((n_rows, 2 * dn), BF16),
        scratch_shapes=[scratch],
        compiler_params=_params(("arbitrary",)),
        name=f"mixer{layer}_{'sample' if sample else 'prompt'}",
    )(p, hist, u, v, wm, bias, w_pool, pool_scale[:, None, :])


def _outproj_body(*refs, mix_offs, layout):
    n_mix = len(mix_offs)
    n_h = sum(len(offs) for offs in layout)
    w_ref = refs[n_mix]
    h_refs = refs[n_mix + 1:n_mix + 1 + n_h]
    o_ref, wbf_ref = refs[n_mix + 1 + n_h:]
    m = pl.program_id(1)

    @pl.when(m == 0)
    def _():
        wbf_ref[...] = w_ref[...].astype(BF16)

    a = _stack_read(refs[:n_mix], mix_offs, m)
    o_ref[...] = _sum_read(h_refs, layout, m) + jnp.dot(a, wbf_ref[...], preferred_element_type=F32)


def _outproj(layer, mix, w_out, addends):
    k = mix[0].shape[1]
    d = w_out.shape[2]
    dn = d // 2
    mix_offs, n_blk = _stack_offsets(mix)
    layout = tuple(_stack_offsets(st)[0] for st in addends)
    h_specs = [sp for st in addends for sp in _stack_specs(st, dn, 1, col_fn=lambda n, m: n)]
    return pl.pallas_call(
        functools.partial(_outproj_body, mix_offs=mix_offs, layout=layout),
        grid=(2, n_blk),
        in_specs=_stack_specs(mix, k, 1) + [
            pl.BlockSpec((None, k, dn), lambda n, m: (layer, 0, n)),
        ] + h_specs,
        out_specs=pl.BlockSpec((ROW_TILE, dn), lambda n, m: (m, n)),
        out_shape=jax.ShapeDtypeStruct((n_blk * ROW_TILE, d), F32),
        scratch_shapes=[pltpu.VMEM((k, dn), BF16)],
        compiler_params=_params(("arbitrary", "arbitrary")),
        name=f"outproj{layer}",
    )(*mix, w_out, *[a for st in addends for a in st])


def _norm_body(h_ref, g_ref, o_ref):
    o_ref[...] = _rms(h_ref[...], g_ref[...]).astype(BF16)


def _norm(layer, h, g):
    t, d = h.shape
    return pl.pallas_call(
        _norm_body,
        grid=(t // ROW_TILE,),
        in_specs=[pl.BlockSpec((ROW_TILE, d), lambda i: (i, 0)),
                  pl.BlockSpec((None, 1, d), lambda i: (layer, 0, 0))],
        out_specs=pl.BlockSpec((ROW_TILE, d), lambda i: (i, 0)),
        out_shape=jax.ShapeDtypeStruct((t, d), BF16),
        compiler_params=_params(("arbitrary",)),
        name=f"norm{layer}",
    )(h, g[:, None, :])


def _router_body(h_ref, g_ref, wr_ref, xn_ref, rank_ref, col_ref, cin_ref, cnt_ref, carry_ref):
    tb = h_ref.shape[0]

    @pl.when(pl.program_id(0) == 0)
    def _():
        carry_ref[...] = jnp.zeros_like(carry_ref)

    xn = _rms(h_ref[...], g_ref[...])
    xn_ref[...] = xn.astype(BF16)
    logits = lax.dot_general(wr_ref[...], xn, (((1,), (1,)), ((), ())),
                             precision=lax.Precision.HIGHEST, preferred_element_type=F32)
    eidx = lax.broadcasted_iota(I32, logits.shape, 0)
    m1 = jnp.max(logits, axis=0, keepdims=True)
    i1 = jnp.min(jnp.where(logits == m1, eidx, N_EXPERTS), axis=0, keepdims=True)
    sel1 = eidx == i1
    rest = jnp.where(sel1, -jnp.inf, logits)
    m2 = jnp.max(rest, axis=0, keepdims=True)
    i2 = jnp.min(jnp.where(rest == m2, eidx, N_EXPERTS), axis=0, keepdims=True)
    sel2 = eidx == i2
    e2 = jnp.exp(m2 - m1)
    g1 = 1.0 / (1.0 + e2)
    g2 = e2 / (1.0 + e2)
    gate = jnp.where(sel1, g1, jnp.where(sel2, g2, 0.0))
    routed = jnp.where(sel1, 1.0, jnp.where(sel2, 1.0, 0.0))
    before = lax.broadcasted_iota(I32, (tb, tb), 0) < lax.broadcasted_iota(I32, (tb, tb), 1)
    upper = jnp.where(before, 1.0, 0.0).astype(BF16)
    excl = jnp.dot(routed.astype(BF16), upper, preferred_element_type=F32)
    carry = carry_ref[...]
    rank = jnp.where(routed > 0.0, excl + carry[:, 0:1], -1.0)
    rank_ref[...] = rank.astype(I32)
    same = lax.broadcasted_iota(I32, (tb, tb), 0) == lax.broadcasted_iota(I32, (tb, tb), 1)
    col_ref[...] = lax.dot_general(jnp.where(same, 1.0, 0.0), jnp.concatenate([rank, gate], axis=0),
                                   (((1,), (1,)), ((), ())),
                                   precision=lax.Precision.HIGHEST, preferred_element_type=F32)
    cin_ref[...] = carry.astype(I32)
    new_carry = carry + jnp.sum(routed, axis=1, keepdims=True)
    carry_ref[...] = new_carry
    cnt_ref[...] = new_carry.astype(I32)


def _router(layer, h, g, w_router_t):
    t, d = h.shape
    nb = t // ROW_TILE
    e = w_router_t.shape[0]
    return pl.pallas_call(
        _router_body,
        grid=(nb,),
        in_specs=[pl.BlockSpec((ROW_TILE, d), lambda i: (i, 0)),
                  pl.BlockSpec((None, 1, d), lambda i: (layer, 0, 0)),
                  pl.BlockSpec((e, d), lambda i: (0, 0))],
        out_specs=[pl.BlockSpec((ROW_TILE, d), lambda i: (i, 0)),
                   pl.BlockSpec((e, ROW_TILE), lambda i: (0, i)),
                   pl.BlockSpec((ROW_TILE, 2 * e), lambda i: (i, 0)),
                   pl.BlockSpec((None, e, 128), lambda i: (i, 0, 0)),
                   pl.BlockSpec((e, 128), lambda i: (0, 0))],
        out_shape=[jax.ShapeDtypeStruct((t, d), BF16),
                   jax.ShapeDtypeStruct((e, t), I32),
                   jax.ShapeDtypeStruct((t, 2 * e), F32),
                   jax.ShapeDtypeStruct((nb, e, 128), I32),
                   jax.ShapeDtypeStruct((e, 128), I32)],
        scratch_shapes=[pltpu.VMEM((e, 128), F32)],
        compiler_params=_params(("arbitrary",)),
        name=f"router{layer}",
    )(h, g[:, None, :], w_router_t)


def _swiglu_blocks(x_ref, wg_ref, wu_ref, wd_ref, acc_ref, nsub):
    def rows_block(first_sub, n_sub):
        n_rows = n_sub * SUB_ROWS
        r0 = first_sub * SUB_ROWS
        if not isinstance(r0, int):
            r0 = pl.multiple_of(r0, SUB_ROWS)
        x = x_ref[pl.ds(r0, n_rows), :]
        g = jnp.dot(x, wg_ref[...].astype(BF16), preferred_element_type=F32)
        u = jnp.dot(x, wu_ref[...].astype(BF16), preferred_element_type=F32)
        a = (g * jax.nn.sigmoid(g) * u).astype(BF16)
        acc_ref[pl.ds(r0, n_rows), :] += jnp.dot(a, wd_ref[...].astype(BF16),
                                                 preferred_element_type=F32)

    if isinstance(nsub, int):
        for m in range(nsub // 4):
            rows_block(m * 4, 4)
        if nsub % 4 >= 2:
            rows_block((nsub // 4) * 4, 2)
        if nsub % 2 == 1:
            rows_block(nsub - 1, 1)
        return

    def quad(m, carry):
        rows_block(m * 4, 4)
        return carry

    lax.fori_loop(0, nsub // 4, quad, 0)

    @pl.when(nsub % 4 >= 2)
    def _():
        rows_block((nsub // 4) * 4, 2)

    @pl.when(nsub % 2 == 1)
    def _():
        rows_block(nsub - 1, 1)


def _dense_ffn_body(x_ref, wg_ref, wu_ref, wd_ref, o_ref):
    @pl.when(pl.program_id(1) == 0)
    def _():
        o_ref[...] = jnp.zeros_like(o_ref)

    _swiglu_blocks(x_ref, wg_ref, wu_ref, wd_ref, o_ref, x_ref.shape[0] // SUB_ROWS)


def _dense_ffn(layer, x, wg, wu, wd):
    rows, d = x.shape
    nf = wg.shape[2] // FF_TILE
    return pl.pallas_call(
        _dense_ffn_body,
        grid=(rows // DENSE_TILE, nf),
        in_specs=[pl.BlockSpec((DENSE_TILE, d), lambda s, f: (s, 0), pipeline_mode=pl.Buffered(1)),
                  pl.BlockSpec((None, d, FF_TILE), lambda s, f: (layer, 0, f)),
                  pl.BlockSpec((None, d, FF_TILE), lambda s, f: (layer, 0, f)),
                  pl.BlockSpec((None, FF_TILE, d), lambda s, f: (layer, f, 0))],
        out_specs=pl.BlockSpec((DENSE_TILE, d), lambda s, f: (s, 0), pipeline_mode=pl.Buffered(1)),
        out_shape=jax.ShapeDtypeStruct((rows, d), F32),
        compiler_params=_params(("arbitrary", "arbitrary")),
        name=f"ffn_dense{layer}",
    )(x, wg, wu, wd)


def _moe_ffn_body(te_ref, ns_ref, tk_ref, blo_ref, bhi_ref, cb_ref,
                  rank_ref, xn_hbm, wg_ref, wu_ref, wd_ref, y_hbm,
                  xs_ref, acc_ref, xbuf_ref, in_sem, out_sem, *, n_blocks):
    s = pl.program_id(0)
    f = pl.program_id(1)
    nsub = ns_ref[s]
    e = te_ref[s]
    tile_rows = xs_ref.shape[0]
    tb = xbuf_ref.shape[1]
    rel0 = tk_ref[s] * tile_rows

    def fetch(b, slot):
        src = xn_hbm.at[pl.ds(pl.multiple_of(b * tb, tb), tb), :]
        return pltpu.make_async_copy(src, xbuf_ref.at[slot], in_sem.at[slot])

    @pl.when(f == 0)
    def _():
        acc_ref[...] = jnp.zeros_like(acc_ref)

    @pl.when((f == 0) & (nsub > 0))
    def _():
        blo = blo_ref[s]
        bhi = bhi_ref[s]
        fetch(blo, 0).start()
        xs_ref[...] = jnp.zeros_like(xs_ref)

        def token_block(b, carry):
            slot = (b - blo) % 2
            fetch(b, slot).wait()

            @pl.when(b < bhi)
            def _():
                fetch(b + 1, 1 - slot).start()

            c0 = cb_ref[e * (n_blocks + 1) + b]
            c1 = cb_ref[e * (n_blocks + 1) + b + 1]
            j_lo = jnp.clip((c0 - rel0) // SUB_ROWS, 0, nsub - 1)
            j_hi = jnp.clip((c1 - 1 - rel0) // SUB_ROWS, 0, nsub - 1)
            rank_row = rank_ref[pl.ds(e, 1), pl.ds(pl.multiple_of(b * tb, tb), tb)]
            xb = xbuf_ref.at[slot]

            def sub_block(j, c):
                r0 = pl.multiple_of(j * SUB_ROWS, SUB_ROWS)
                want = lax.broadcasted_iota(I32, (SUB_ROWS, tb), 0) + (rel0 + r0)
                onehot = jnp.where(want == rank_row, 1.0, 0.0).astype(BF16)
                xs_ref[pl.ds(r0, SUB_ROWS), :] += jnp.dot(
                    onehot, xb[...], preferred_element_type=F32).astype(BF16)
                return c

            lax.fori_loop(j_lo, j_hi + 1, sub_block, 0)
            return carry

        lax.fori_loop(blo, bhi + 1, token_block, 0)

    _swiglu_blocks(xs_ref, wg_ref, wu_ref, wd_ref, acc_ref, nsub)

    @pl.when(f == pl.num_programs(1) - 1)
    def _():
        xs_ref[...] = acc_ref[...].astype(BF16)
        dst = y_hbm.at[pl.ds(pl.multiple_of(s * tile_rows, tile_rows), tile_rows), :]
        out = pltpu.make_async_copy(xs_ref, dst, out_sem.at[0])
        out.start()
        out.wait()


def _moe_ffn(layer, xn, rank, plan, wg, wu, wd):
    tile_e, tile_nsub, tile_k, tile_blo, tile_bhi, cblk = plan
    t, d = xn.shape
    ff = wg.shape[3]
    nf = ff // FF_TILE
    n_tiles = tile_e.shape[0]
    nb = t // ROW_TILE

    def w_col(s, f, te, ns, *_):
        return (layer, te[s], 0, jnp.where(ns[s] > 0, f, nf - 1))

    def w_row(s, f, te, ns, *_):
        return (layer, te[s], jnp.where(ns[s] > 0, f, nf - 1), 0)

    return pl.pallas_call(
        functools.partial(_moe_ffn_body, n_blocks=nb),
        grid_spec=pltpu.PrefetchScalarGridSpec(
            num_scalar_prefetch=6,
            grid=(n_tiles, nf),
            in_specs=[pl.BlockSpec(rank.shape, lambda s, f, *_: (0, 0), pipeline_mode=pl.Buffered(1)),
                      pl.BlockSpec(memory_space=pl.ANY),
                      pl.BlockSpec((None, None, d, FF_TILE), w_col),
                      pl.BlockSpec((None, None, d, FF_TILE), w_col),
                      pl.BlockSpec((None, None, FF_TILE, d), w_row)],
            out_specs=pl.BlockSpec(memory_space=pl.ANY),
            scratch_shapes=[pltpu.VMEM((MOE_TILE, d), BF16),
                            pltpu.VMEM((MOE_TILE, d), F32),
                            pltpu.VMEM((2, ROW_TILE, d), BF16),
                            pltpu.SemaphoreType.DMA((2,)),
                            pltpu.SemaphoreType.DMA((1,))]),
        out_shape=jax.ShapeDtypeStruct((n_tiles * MOE_TILE, d), BF16),
        compiler_params=_params(("arbitrary", "arbitrary")),
        name=f"ffn_moe{layer}",
    )(tile_e, tile_nsub, tile_k, tile_blo, tile_bhi, cblk.reshape(-1), rank, xn, wg, wu, wd)


def _combine_body(pb_ref, py_ref, pe_ref, prel_ref, pfirst_ref, plive_ref, plast_ref,
                  col_ref, y_ref, h_ref, gf_ref, op_ref, os_ref, acc_ref, *, prompt_blocks):
    q = pl.program_id(0)
    tb, yr = h_ref.shape[0], y_ref.shape[0]

    @pl.when(pfirst_ref[q] == 1)
    def _():
        acc_ref[...] = h_ref[...]

    @pl.when(plive_ref[q] == 1)
    def _():
        e = pe_ref[q]
        cols = col_ref[...]
        lane = lax.broadcasted_iota(I32, cols.shape, 1)
        rank_col = jnp.sum(jnp.where(lane == e, cols, 0.0), axis=1, keepdims=True)
        gate_col = jnp.sum(jnp.where(lane == e + N_EXPERTS, cols, 0.0), axis=1, keepdims=True)
        want = (lax.broadcasted_iota(I32, (tb, yr), 1) + prel_ref[q]).astype(F32)
        onehot = jnp.where(want == rank_col, 1.0, 0.0).astype(BF16)
        picked = jnp.dot(onehot, y_ref[...], preferred_element_type=F32)
        acc_ref[...] += gate_col * picked

    @pl.when(plast_ref[q] == 1)
    def _():
        res = _rms(acc_ref[...], gf_ref[...])

        @pl.when(pb_ref[q] < prompt_blocks)
        def _():
            op_ref[...] = res

        @pl.when(pb_ref[q] >= prompt_blocks)
        def _():
            os_ref[...] = res


def _combine(h, y, cols, plan, g_final, n_prompt):
    t, d = h.shape
    pb, py, pe, prel, pfirst, plive, plast = plan
    npb = n_prompt // ROW_TILE
    return pl.pallas_call(
        functools.partial(_combine_body, prompt_blocks=npb),
        grid_spec=pltpu.PrefetchScalarGridSpec(
            num_scalar_prefetch=7,
            grid=(pb.shape[0],),
            in_specs=[pl.BlockSpec((ROW_TILE, cols.shape[1]), lambda q, pb, *_: (pb[q], 0)),
                      pl.BlockSpec((SUB_ROWS, d), lambda q, pb, py, *_: (py[q], 0)),
                      pl.BlockSpec((ROW_TILE, d), lambda q, pb, *_: (pb[q], 0)),
                      pl.BlockSpec((1, d), lambda q, *_: (0, 0))],
            out_specs=[pl.BlockSpec((ROW_TILE, d), lambda q, pb, *_: (jnp.minimum(pb[q], npb - 1), 0)),
                       pl.BlockSpec((ROW_TILE, d), lambda q, pb, *_: (jnp.maximum(pb[q] - npb, 0), 0))],
            scratch_shapes=[pltpu.VMEM((ROW_TILE, d), F32)]),
        out_shape=[jax.ShapeDtypeStruct((n_prompt, d), F32),
                   jax.ShapeDtypeStruct((t - n_prompt, d), F32)],
        compiler_params=_params(("arbitrary",)),
        name="moe_combine",
    )(pb, py, pe, prel, pfirst, plive, plast, cols, y, h, g_final[None, :])


def _owner(cum, q):
    return jnp.sum(cum[None, :] <= q[:, None], axis=1).astype(I32)


def _moe_plan(cin, cnt, n_tok):
    nb = n_tok // ROW_TILE
    n_slots = n_tok * TOP_K
    s_max = n_slots // MOE_TILE + N_EXPERTS
    counts = cnt[:, 0]
    cblk = jnp.concatenate([cin[:, :, 0].T, counts[:, None]], axis=1)
    ntile = (counts + MOE_TILE - 1) // MOE_TILE
    tcum = jnp.cumsum(ntile)
    tstart = tcum - ntile
    total = tcum[-1]
    s_idx = jnp.arange(s_max, dtype=I32)
    tile_ok = s_idx < total
    tile_e = _owner(tcum, jnp.minimum(s_idx, total - 1))
    tile_k = s_idx - tstart[tile_e]
    tile_rows = jnp.where(tile_ok, jnp.clip(counts[tile_e] - tile_k * MOE_TILE, 0, MOE_TILE), 0)
    tile_nsub = ((tile_rows + SUB_ROWS - 1) // SUB_ROWS).astype(I32)
    region = (tstart * MOE_TILE).astype(I32)

    rel = tile_k * MOE_TILE
    cb = cblk[tile_e]
    b_lo = jnp.clip(jnp.sum(cb[:, 1:] <= rel[:, None], axis=1), 0, nb - 1)
    b_hi = jnp.sum(cb[:, :-1] < (rel + tile_rows)[:, None], axis=1) - 1
    b_hi = jnp.clip(jnp.maximum(b_hi, b_lo), 0, nb - 1)
    fplan = (tile_e, tile_nsub, tile_k.astype(I32), b_lo.astype(I32), b_hi.astype(I32),
             cblk.astype(I32))

    span = cblk[:, 1:] - cblk[:, :-1]
    start = region[:, None] + cblk[:, :-1]
    r_lo = start // SUB_ROWS
    r_hi = (start + span - 1) // SUB_ROWS
    ncomb = jnp.where(span > 0, r_hi - r_lo + 1, 0).T.reshape(-1)
    r_lo = r_lo.T.reshape(-1)
    ccum = jnp.cumsum(ncomb)
    cstart = ccum - ncomb
    n_c = nb * N_EXPERTS * (ROW_TILE // SUB_ROWS + 1)
    q = jnp.arange(n_c, dtype=I32)
    alive = q < ccum[-1]
    qc = jnp.minimum(q, ccum[-1] - 1)
    be = _owner(ccum, qc)
    kq = qc - cstart[be]
    pb = be // N_EXPERTS
    pe = be % N_EXPERTS
    py = (r_lo[be] + kq).astype(I32)
    prel = py * SUB_ROWS - region[pe]
    prev_b = jnp.concatenate([jnp.full((1,), -1, I32), pb[:-1]])
    next_b = jnp.concatenate([pb[1:], jnp.full((1,), -1, I32)])
    next_alive = jnp.concatenate([alive[1:], jnp.zeros((1,), bool)])
    first = alive & (pb != prev_b)
    last = alive & ((pb != next_b) | ~next_alive)
    cplan = (pb.astype(I32), py, pe.astype(I32), prel.astype(I32), first.astype(I32),
             alive.astype(I32), last.astype(I32))
    return fplan, cplan


def kernel(x_prompt, x_sample, state_pool, g_mix, w_in, g_v, w_pool, pool_scale, w_s, b_s, w_out,
           g_ffn, dense_w_gate, dense_w_up, dense_w_down, w_router, moe_w_gate, moe_w_up,
           moe_w_down, g_final):
    batch, seq, d = x_prompt.shape
    dec_batch, dec_seq, _ = x_sample.shape
    depth = w_in.shape[0]
    n_p = batch * seq
    n_s = dec_batch * dec_seq
    n_tok = n_p + n_s
    assert dec_seq == 8 and depth == 2 and n_p % ROW_TILE == 0 and n_s % ROW_TILE == 0

    addends = ((x_prompt.reshape(n_p, d), x_sample.reshape(n_s, d)),)
    dn = w_in.shape[2] // 3
    hd = dn // N_HEADS
    per_chunk = CHUNK // dec_seq
    pools_p, pools_s, vs_s = [], [], []
    out = None
    for i in range(depth):
        p, u, v = _inproj(i, addends, g_mix, w_in, g_v)
        bias_p = jnp.repeat(b_s[i].T, hd, axis=1)
        ws_s = w_s[i][:, :dec_seq, :dec_seq]
        eye = jnp.eye(per_chunk, dtype=F32)
        wm_s = (eye[None, :, None, :, None] * ws_s[:, None, :, None, :]).reshape(N_HEADS, CHUNK, CHUNK)
        bias_s = jnp.tile(jnp.repeat(b_s[i][:, :dec_seq].T, hd, axis=1), (per_chunk, 1))
        mix_p = _mixer(i, p, u, v, 0, n_p, seq, 0, w_s[i], bias_p, w_pool, pool_scale)
        mix_s = _mixer(i, p, u, v, n_p, n_s, dec_seq, PAST_LEN, wm_s, bias_s, w_pool, pool_scale,
                       state=state_pool)
        h = _outproj(i, (mix_p, mix_s), w_out, addends)

        pools_p.append(jnp.stack([p[(b + 1) * seq - POOL_BUF:(b + 1) * seq] for b in range(batch)]))
        pools_s.append(jnp.concatenate(
            [state_pool[i, :, dec_seq:], p[n_p:].reshape(dec_batch, dec_seq, dn)], axis=1))
        vs_s.append(v[n_p:].reshape(dec_batch, dec_seq, dn))

        j = i // 2
        if i % 2 == 0:
            xn = _norm(i, h, g_ffn)
            y = _dense_ffn(j, xn, dense_w_gate, dense_w_up, dense_w_down)
            addends = ((h,), (y,))
        else:
            xn, rank, cols, cin, cnt = _router(i, h, g_ffn, w_router[j].T)
            fplan, cplan = _moe_plan(cin, cnt, n_tok)
            y = _moe_ffn(j, xn, rank, fplan, moe_w_gate, moe_w_up, moe_w_down)
            out = _combine(h, y, cols, cplan, g_final, n_p)
            addends = (tuple(out),)

    y_prompt = out[0].reshape(batch, seq, d)
    y_sample = out[1].reshape(dec_batch, dec_seq, d)
    return (y_prompt, y_sample, jnp.stack(pools_p), jnp.stack(pools_s), jnp.stack(vs_s))
```

```python
import functools

import jax
import jax.numpy as jnp
from jax import lax
from jax.experimental import pallas as pl
from jax.experimental.pallas import tpu as pltpu

F32 = jnp.float32
BF16 = jnp.bfloat16
I32 = jnp.int32

EPS = 1e-6
POOL_WINDOWS = (2, 4, 8, 16)
POOL_BUF = max(POOL_WINDOWS) - 1
HALO = 16
CHUNK = 128
N_HEADS = 8
N_EXPERTS = 8
TOP_K = 2
PAST_LEN = 16384

V7X_VMEM_LIMIT = 56 * 1024 * 1024
V7X_MXU_COLS = 256

ROW_TILE = 512
FF_TILE = 256
SUB_ROWS = 256
MOE_TILE = 2560
WINDOW_ROWS = 256
BF16_SUBLANES = 16
DENSE_TILE = 2304


def _params(sem):
    return pltpu.CompilerParams(dimension_semantics=sem, vmem_limit_bytes=V7X_VMEM_LIMIT)


def _rms(x, g):
    return x * lax.rsqrt(jnp.mean(x * x, axis=-1, keepdims=True) + EPS) * g


def _gelu(x):
    return 0.5 * x * (1.0 + lax.erf(x * (2.0 ** -0.5)))


def _stack_offsets(stack):
    offs, o = [], 0
    for a in stack:
        offs.append(o)
        o += a.shape[0] // ROW_TILE
    return tuple(offs), o


def _stack_specs(stack, cols, m_axis, col_fn=None):
    offs, _ = _stack_offsets(stack)
    specs = []
    for a, off in zip(stack, offs):
        nblk = a.shape[0] // ROW_TILE

        def imap(*idx, off=off, nblk=nblk):
            col = 0 if col_fn is None else col_fn(*idx)
            return (jnp.clip(idx[m_axis] - off, 0, nblk - 1), col)

        specs.append(pl.BlockSpec((ROW_TILE, cols), imap))
    return specs


def _stack_read(refs, offs, m):
    v = refs[0][...]
    for ref, off in zip(refs[1:], offs[1:]):
        v = jnp.where(m >= off, ref[...], v)
    return v


def _sum_read(refs, layout, m):
    total, k = None, 0
    for offs in layout:
        part = _stack_read(refs[k:k + len(offs)], offs, m)
        k += len(offs)
        total = part if total is None else total + part
    return total


def _inproj_body(*refs, layout, kind):
    n_x = sum(len(offs) for offs in layout)
    g_ref, w_ref, gv_ref, o_ref, wbf_ref = refs[n_x:]
    m = pl.program_id(0)

    @pl.when(m == 0)
    def _():
        wbf_ref[...] = w_ref[...].astype(BF16)

    xn = _rms(_sum_read(refs[:n_x], layout, m), g_ref[...]).astype(BF16)
    dn = o_ref.shape[1]
    ss = None
    for c0 in range(0, dn, V7X_MXU_COLS):
        acc = jnp.dot(xn, wbf_ref[:, c0:c0 + V7X_MXU_COLS], preferred_element_type=F32)
        if kind != "pool":
            acc = _gelu(acc)
        if kind == "v":
            part = jnp.sum(acc * acc, axis=1, keepdims=True)
            ss = part if ss is None else ss + part
        o_ref[:, c0:c0 + V7X_MXU_COLS] = acc
    if kind == "v":
        o_ref[...] = o_ref[...] * lax.rsqrt(ss / dn + EPS) * gv_ref[...]


def _inproj(layer, addends, g_mix, w_in, g_v):
    d = addends[0][0].shape[1]
    dn = w_in.shape[2] // 3
    layout = tuple(_stack_offsets(st)[0] for st in addends)
    n_blk = _stack_offsets(addends[0])[1]
    x_specs = [sp for st in addends for sp in _stack_specs(st, d, 0)]
    outs = []
    for k, kind in enumerate(("pool", "u", "v")):
        outs.append(pl.pallas_call(
            functools.partial(_inproj_body, layout=layout, kind=kind),
            grid=(n_blk,),
            in_specs=x_specs + [
                pl.BlockSpec((None, 1, d), lambda m: (layer, 0, 0)),
                pl.BlockSpec((None, d, dn), lambda m, k=k: (layer, 0, k), pipeline_mode=pl.Buffered(1)),
                pl.BlockSpec((None, 1, dn), lambda m: (layer, 0, 0)),
            ],
            out_specs=pl.BlockSpec((ROW_TILE, dn), lambda m: (m, 0)),
            out_shape=jax.ShapeDtypeStruct((n_blk * ROW_TILE, dn), F32),
            scratch_shapes=[pltpu.VMEM((d, dn), BF16)],
            compiler_params=_params(("arbitrary",)),
            name=f"inproj{layer}_{kind}",
        )(*[a for st in addends for a in st], g_mix[:, None, :], w_in, g_v[:, None, :]))
    return outs


def _mixer_body(*refs, rows, blocks_per_seq, start_pos, sample):
    p_ref, hist_ref, u_ref, v_ref, wm_ref, bias_ref, wp_ref, ps_ref, o_ref, full_ref = refs
    d_pool = p_ref.shape[1]
    gd = d_pool // len(POOL_WINDOWS)
    hd = u_ref.shape[1] // N_HEADS
    blk = pl.program_id(0) % blocks_per_seq
    p = p_ref[...]

    if sample:
        seqs = rows // 8
        full_ref[:, 1:HALO, :] = hist_ref[...]
        full_ref[:, HALO:HALO + 8, :] = p.reshape(seqs, 8, d_pool)
    else:
        full_ref[0:HALO, :] = jnp.where(blk == 0, 0.0, hist_ref[...])
        full_ref[HALO:HALO + rows, :] = p
    row_id = lax.broadcasted_iota(I32, (rows, 1), 0)
    pos = start_pos + (row_id % 8 if sample else blk * rows + row_id)
    for gi, w in enumerate(POOL_WINDOWS):
        c0 = gi * gd
        s = None
        for j in range(w):
            if sample:
                term = full_ref[:, HALO - j:HALO - j + 8, c0:c0 + gd].reshape(rows, gd)
            else:
                term = full_ref[HALO - j:HALO - j + rows, c0:c0 + gd]
            s = term if s is None else s + term
        cnt = jnp.minimum(pos + 1, w).astype(F32)
        r = s / cnt - p[:, c0:c0 + gd]
        y = jnp.dot(r.astype(BF16), wp_ref[gi].astype(BF16), preferred_element_type=F32)
        o_ref[:, c0:c0 + gd] = (y * ps_ref[:, c0:c0 + gd]).astype(BF16)

    row = lax.broadcasted_iota(I32, (CHUNK, CHUNK), 0)
    col = lax.broadcasted_iota(I32, (CHUNK, CHUNK), 1)
    for h in range(N_HEADS):
        c0 = h * hd
        wm = jnp.where(row >= col, wm_ref[h], 0.0).astype(BF16)
        for c in range(rows // CHUNK):
            r0 = c * CHUNK
            z = jnp.dot(wm, v_ref[r0:r0 + CHUNK, c0:c0 + hd].astype(BF16),
                        preferred_element_type=F32) + bias_ref[:, c0:c0 + hd]
            o_ref[r0:r0 + CHUNK, d_pool + c0:d_pool + c0 + hd] = (
                u_ref[r0:r0 + CHUNK, c0:c0 + hd] * z).astype(BF16)


def _mixer(layer, p, u, v, row0, n_rows, seq_len, start_pos, wm, bias, w_pool, pool_scale,
           state=None):
    dn = p.shape[1]
    sample = state is not None
    rows = ROW_TILE
    blocks_per_seq = 1 if sample else seq_len // rows
    b0 = row0 // rows
    blk_spec = pl.BlockSpec((rows, dn), lambda i: (b0 + i, 0))
    if sample:
        hist_spec = pl.BlockSpec((None, rows // 8, POOL_BUF, dn), lambda i: (layer, i, 0, 0))
        hist = state
        scratch = pltpu.VMEM((rows // 8, HALO + 8, dn), F32)
    else:
        per = rows // HALO
        hist_spec = pl.BlockSpec((HALO, dn), lambda i: (jnp.maximum((b0 + i) * per - 1, 0), 0))
        hist = p
        scratch = pltpu.VMEM((HALO + rows, dn), F32)
    return pl.pallas_call(
        functools.partial(_mixer_body, rows=rows, blocks_per_seq=blocks_per_seq,
                          start_pos=start_pos, sample=sample),
        grid=(n_rows // rows,),
        in_specs=[
            blk_spec, hist_spec, blk_spec, blk_spec,
            pl.BlockSpec((N_HEADS, CHUNK, CHUNK), lambda i: (0, 0, 0)),
            pl.BlockSpec((CHUNK, dn), lambda i: (0, 0)),
            pl.BlockSpec((None,) + w_pool.shape[1:], lambda i: (layer, 0, 0, 0)),
            pl.BlockSpec((None, 1, dn), lambda i: (layer, 0, 0)),
        ],
        out_specs=pl.BlockSpec((rows, 2 * dn), lambda i: (i, 0)),
        out_shape=jax.ShapeDtypeStruct((n_rows, 2 * dn), BF16),
        scratch_shapes=[scratch],
        compiler_params=_params(("arbitrary",)),
        name=f"mixer{layer}_{'sample' if sample else 'prompt'}",
    )(p, hist, u, v, wm, bias, w_pool, pool_scale[:, None, :])


def _outproj_body(*refs, mix_offs, layout):
    n_mix = len(mix_offs)
    n_h = sum(len(offs) for offs in layout)
    w_ref = refs[n_mix]
    h_refs = refs[n_mix + 1:n_mix + 1 + n_h]
    o_ref, wbf_ref = refs[n_mix + 1 + n_h:]
    m = pl.program_id(1)

    @pl.when(m == 0)
    def _():
        wbf_ref[...] = w_ref[...].astype(BF16)

    a = _stack_read(refs[:n_mix], mix_offs, m)
    o_ref[...] = _sum_read(h_refs, layout, m) + jnp.dot(a, wbf_ref[...], preferred_element_type=F32)


def _outproj(layer, mix, w_out, addends):
    k = mix[0].shape[1]
    d = w_out.shape[2]
    dn = d // 2
    mix_offs, n_blk = _stack_offsets(mix)
    layout = tuple(_stack_offsets(st)[0] for st in addends)
    h_specs = [sp for st in addends for sp in _stack_specs(st, dn, 1, col_fn=lambda n, m: n)]
    return pl.pallas_call(
        functools.partial(_outproj_body, mix_offs=mix_offs, layout=layout),
        grid=(2, n_blk),
        in_specs=_stack_specs(mix, k, 1) + [
            pl.BlockSpec((None, k, dn), lambda n, m: (layer, 0, n)),
        ] + h_specs,
        out_specs=pl.BlockSpec((ROW_TILE, dn), lambda n, m: (m, n)),
        out_shape=jax.ShapeDtypeStruct((n_blk * ROW_TILE, d), F32),
        scratch_shapes=[pltpu.VMEM((k, dn), BF16)],
        compiler_params=_params(("arbitrary", "arbitrary")),
        name=f"outproj{layer}",
    )(*mix, w_out, *[a for st in addends for a in st])


def _norm_body(h_ref, g_ref, o_ref):
    o_ref[...] = _rms(h_ref[...], g_ref[...]).astype(BF16)


def _norm(layer, h, g):
    t, d = h.shape
    return pl.pallas_call(
        _norm_body,
        grid=(t // ROW_TILE,),
        in_specs=[pl.BlockSpec((ROW_TILE, d), lambda i: (i, 0)),
                  pl.BlockSpec((None, 1, d), lambda i: (layer, 0, 0))],
        out_specs=pl.BlockSpec((ROW_TILE, d), lambda i: (i, 0)),
        out_shape=jax.ShapeDtypeStruct((t, d), BF16),
        compiler_params=_params(("arbitrary",)),
        name=f"norm{layer}",
    )(h, g[:, None, :])


def _router_body(h_ref, g_ref, wr_ref, xn_ref, rank_ref, col_ref, cin_ref, cnt_ref, carry_ref):
    tb = h_ref.shape[0]

    @pl.when(pl.program_id(0) == 0)
    def _():
        carry_ref[...] = jnp.zeros_like(carry_ref)

    xn = _rms(h_ref[...], g_ref[...])
    xn_ref[...] = xn.astype(BF16)
    logits = lax.dot_general(wr_ref[...], xn, (((1,), (1,)), ((), ())),
                             precision=lax.Precision.HIGHEST, preferred_element_type=F32)
    eidx = lax.broadcasted_iota(I32, logits.shape, 0)
    m1 = jnp.max(logits, axis=0, keepdims=True)
    i1 = jnp.min(jnp.where(logits == m1, eidx, N_EXPERTS), axis=0, keepdims=True)
    sel1 = eidx == i1
    rest = jnp.where(sel1, -jnp.inf, logits)
    m2 = jnp.max(rest, axis=0, keepdims=True)
    i2 = jnp.min(jnp.where(rest == m2, eidx, N_EXPERTS), axis=0, keepdims=True)
    sel2 = eidx == i2
    e2 = jnp.exp(m2 - m1)
    g1 = 1.0 / (1.0 + e2)
    g2 = e2 / (1.0 + e2)
    gate = jnp.where(sel1, g1, jnp.where(sel2, g2, 0.0))
    routed = jnp.where(sel1, 1.0, jnp.where(sel2, 1.0, 0.0))
    before = lax.broadcasted_iota(I32, (tb, tb), 0) < lax.broadcasted_iota(I32, (tb, tb), 1)
    upper = jnp.where(before, 1.0, 0.0).astype(BF16)
    excl = jnp.dot(routed.astype(BF16), upper, preferred_element_type=F32)
    carry = carry_ref[...]
    rank = jnp.where(routed > 0.0, excl + carry[:, 0:1], -1.0)
    rank_ref[...] = rank.astype(I32)
    same = lax.broadcasted_iota(I32, (tb, tb), 0) == lax.broadcasted_iota(I32, (tb, tb), 1)
    col_ref[...] = lax.dot_general(jnp.where(same, 1.0, 0.0), jnp.concatenate([rank, gate], axis=0),
                                   (((1,), (1,)), ((), ())),
                                   precision=lax.Precision.HIGHEST, preferred_element_type=F32)
    cin_ref[...] = carry.astype(I32)
    new_carry = carry + jnp.sum(routed, axis=1, keepdims=True)
    carry_ref[...] = new_carry
    cnt_ref[...] = new_carry.astype(I32)


def _router(layer, h, g, w_router_t):
    t, d = h.shape
    nb = t // ROW_TILE
    e = w_router_t.shape[0]
    return pl.pallas_call(
        _router_body,
        grid=(nb,),
        in_specs=[pl.BlockSpec((ROW_TILE, d), lambda i: (i, 0)),
                  pl.BlockSpec((None, 1, d), lambda i: (layer, 0, 0)),
                  pl.BlockSpec((e, d), lambda i: (0, 0))],
        out_specs=[pl.BlockSpec((ROW_TILE, d), lambda i: (i, 0)),
                   pl.BlockSpec((e, ROW_TILE), lambda i: (0, i)),
                   pl.BlockSpec((ROW_TILE, 2 * e), lambda i: (i, 0)),
                   pl.BlockSpec((None, e, 128), lambda i: (i, 0, 0)),
                   pl.BlockSpec((e, 128), lambda i: (0, 0))],
        out_shape=[jax.ShapeDtypeStruct((t, d), BF16),
                   jax.ShapeDtypeStruct((e, t), I32),
                   jax.ShapeDtypeStruct((t, 2 * e), F32),
                   jax.ShapeDtypeStruct((nb, e, 128), I32),
                   jax.ShapeDtypeStruct((e, 128), I32)],
        scratch_shapes=[pltpu.VMEM((e, 128), F32)],
        compiler_params=_params(("arbitrary",)),
        name=f"router{layer}",
    )(h, g[:, None, :], w_router_t)


def _swiglu_blocks(x_ref, wg_ref, wu_ref, wd_ref, acc_ref, nsub):
    def rows_block(first_sub, n_sub):
        n_rows = n_sub * SUB_ROWS
        r0 = first_sub * SUB_ROWS
        if not isinstance(r0, int):
            r0 = pl.multiple_of(r0, SUB_ROWS)
        x = x_ref[pl.ds(r0, n_rows), :]
        g = jnp.dot(x, wg_ref[...].astype(BF16), preferred_element_type=F32)
        u = jnp.dot(x, wu_ref[...].astype(BF16), preferred_element_type=F32)
        a = (g * jax.nn.sigmoid(g) * u).astype(BF16)
        acc_ref[pl.ds(r0, n_rows), :] += jnp.dot(a, wd_ref[...].astype(BF16),
                                                 preferred_element_type=F32)

    if isinstance(nsub, int):
        for m in range(nsub // 4):
            rows_block(m * 4, 4)
        if nsub % 4 >= 2:
            rows_block((nsub // 4) * 4, 2)
        if nsub % 2 == 1:
            rows_block(nsub - 1, 1)
        return

    def quad(m, carry):
        rows_block(m * 4, 4)
        return carry

    lax.fori_loop(0, nsub // 4, quad, 0)

    @pl.when(nsub % 4 >= 2)
    def _():
        rows_block((nsub // 4) * 4, 2)

    @pl.when(nsub % 2 == 1)
    def _():
        rows_block(nsub - 1, 1)


def _dense_ffn_body(x_ref, wg_ref, wu_ref, wd_ref, o_ref):
    @pl.when(pl.program_id(1) == 0)
    def _():
        o_ref[...] = jnp.zeros_like(o_ref)

    _swiglu_blocks(x_ref, wg_ref, wu_ref, wd_ref, o_ref, x_ref.shape[0] // SUB_ROWS)


def _dense_ffn(layer, x, wg, wu, wd):
    rows, d = x.shape
    nf = wg.shape[2] // FF_TILE
    return pl.pallas_call(
        _dense_ffn_body,
        grid=(rows // DENSE_TILE, nf),
        in_specs=[pl.BlockSpec((DENSE_TILE, d), lambda s, f: (s, 0), pipeline_mode=pl.Buffered(1)),
                  pl.BlockSpec((None, d, FF_TILE), lambda s, f: (layer, 0, f)),
                  pl.BlockSpec((None, d, FF_TILE), lambda s, f: (layer, 0, f)),
                  pl.BlockSpec((None, FF_TILE, d), lambda s, f: (layer, f, 0))],
        out_specs=pl.BlockSpec((DENSE_TILE, d), lambda s, f: (s, 0), pipeline_mode=pl.Buffered(1)),
        out_shape=jax.ShapeDtypeStruct((rows, d), F32),
        compiler_params=_params(("arbitrary", "arbitrary")),
        name=f"ffn_dense{layer}",
    )(x, wg, wu, wd)


def _moe_ffn_body(te_ref, ns_ref, tk_ref, blo_ref, bhi_ref, cb_ref,
                  rank_ref, xn_hbm, wg_ref, wu_ref, wd_ref, y_hbm,
                  xs_ref, acc_ref, xbuf_ref, in_sem, out_sem, *, n_blocks):
    s = pl.program_id(0)
    f = pl.program_id(1)
    nsub = ns_ref[s]
    e = te_ref[s]
    tile_rows = xs_ref.shape[0]
    tb = xbuf_ref.shape[1]
    rel0 = tk_ref[s] * tile_rows

    def fetch(b, slot):
        src = xn_hbm.at[pl.ds(pl.multiple_of(b * tb, tb), tb), :]
        return pltpu.make_async_copy(src, xbuf_ref.at[slot], in_sem.at[slot])

    @pl.when(f == 0)
    def _():
        acc_ref[...] = jnp.zeros_like(acc_ref)

    @pl.when((f == 0) & (nsub > 0))
    def _():
        blo = blo_ref[s]
        bhi = bhi_ref[s]
        fetch(blo, 0).start()
        xs_ref[...] = jnp.zeros_like(xs_ref)

        def token_block(b, carry):
            slot = (b - blo) % 2
            fetch(b, slot).wait()

            @pl.when(b < bhi)
            def _():
                fetch(b + 1, 1 - slot).start()

            c0 = cb_ref[e * (n_blocks + 1) + b]
            c1 = cb_ref[e * (n_blocks + 1) + b + 1]
            j_lo = jnp.clip((c0 - rel0) // SUB_ROWS, 0, nsub - 1)
            j_hi = jnp.clip((c1 - 1 - rel0) // SUB_ROWS, 0, nsub - 1)
            rank_row = rank_ref[pl.ds(e, 1), pl.ds(pl.multiple_of(b * tb, tb), tb)]
            xb = xbuf_ref.at[slot]

            def sub_block(j, c):
                r0 = pl.multiple_of(j * SUB_ROWS, SUB_ROWS)
                want = lax.broadcasted_iota(I32, (SUB_ROWS, tb), 0) + (rel0 + r0)
                onehot = jnp.where(want == rank_row, 1.0, 0.0).astype(BF16)
                xs_ref[pl.ds(r0, SUB_ROWS), :] += jnp.dot(
                    onehot, xb[...], preferred_element_type=F32).astype(BF16)
                return c

            lax.fori_loop(j_lo, j_hi + 1, sub_block, 0)
            return carry

        lax.fori_loop(blo, bhi + 1, token_block, 0)

    _swiglu_blocks(xs_ref, wg_ref, wu_ref, wd_ref, acc_ref, nsub)

    @pl.when(f == pl.num_programs(1) - 1)
    def _():
        xs_ref[...] = acc_ref[...].astype(BF16)
        dst = y_hbm.at[pl.ds(pl.multiple_of(s * tile_rows, tile_rows), tile_rows), :]
        out = pltpu.make_async_copy(xs_ref, dst, out_sem.at[0])
        out.start()
        out.wait()

        @pl.when(s == pl.num_programs(0) - 1)
        def _():
            xs_ref[0:WINDOW_ROWS, :] = jnp.zeros((WINDOW_ROWS, xs_ref.shape[1]), BF16)
            end = y_hbm.shape[0] - WINDOW_ROWS
            tail = pltpu.make_async_copy(xs_ref.at[pl.ds(0, WINDOW_ROWS), :],
                                         y_hbm.at[pl.ds(end, WINDOW_ROWS), :], out_sem.at[0])
            tail.start()
            tail.wait()


def _moe_ffn(layer, xn, rank, plan, wg, wu, wd):
    tile_e, tile_nsub, tile_k, tile_blo, tile_bhi, cblk = plan
    t, d = xn.shape
    ff = wg.shape[3]
    nf = ff // FF_TILE
    n_tiles = tile_e.shape[0]
    nb = t // ROW_TILE

    def w_col(s, f, te, ns, *_):
        return (layer, te[s], 0, jnp.where(ns[s] > 0, f, nf - 1))

    def w_row(s, f, te, ns, *_):
        return (layer, te[s], jnp.where(ns[s] > 0, f, nf - 1), 0)

    return pl.pallas_call(
        functools.partial(_moe_ffn_body, n_blocks=nb),
        grid_spec=pltpu.PrefetchScalarGridSpec(
            num_scalar_prefetch=6,
            grid=(n_tiles, nf),
            in_specs=[pl.BlockSpec(rank.shape, lambda s, f, *_: (0, 0), pipeline_mode=pl.Buffered(1)),
                      pl.BlockSpec(memory_space=pl.ANY),
                      pl.BlockSpec((None, None, d, FF_TILE), w_col),
                      pl.BlockSpec((None, None, d, FF_TILE), w_col),
                      pl.BlockSpec((None, None, FF_TILE, d), w_row)],
            out_specs=pl.BlockSpec(memory_space=pl.ANY),
            scratch_shapes=[pltpu.VMEM((MOE_TILE, d), BF16),
                            pltpu.VMEM((MOE_TILE, d), F32),
                            pltpu.VMEM((2, ROW_TILE, d), BF16),
                            pltpu.SemaphoreType.DMA((2,)),
                            pltpu.SemaphoreType.DMA((1,))]),
        out_shape=jax.ShapeDtypeStruct((n_tiles * MOE_TILE + WINDOW_ROWS, d), BF16),
        compiler_params=_params(("arbitrary", "arbitrary")),
        name=f"ffn_moe{layer}",
    )(tile_e, tile_nsub, tile_k, tile_blo, tile_bhi, cblk.reshape(-1), rank, xn, wg, wu, wd)


def _combine_body(nwin_ref, ws_ref, we_ref, region_ref, col_ref, y_hbm, h_ref, gf_ref,
                  op_ref, os_ref, acc_ref, ybuf_ref, sem, *, prompt_blocks, max_windows):
    b = pl.program_id(0)
    tb = h_ref.shape[0]
    n = nwin_ref[b]
    base = b * max_windows

    def fetch(i, slot):
        start = pl.multiple_of(ws_ref[base + i], BF16_SUBLANES)
        return pltpu.make_async_copy(y_hbm.at[pl.ds(start, WINDOW_ROWS), :], ybuf_ref.at[slot],
                                     sem.at[slot])

    @pl.when(n > 0)
    def _():
        fetch(0, 0).start()

    acc_ref[...] = h_ref[...]

    def window(i, carry):
        slot = i % 2
        fetch(i, slot).wait()

        @pl.when(i + 1 < n)
        def _():
            fetch(i + 1, 1 - slot).start()

        e = we_ref[base + i]
        cols = col_ref[...]
        lane = lax.broadcasted_iota(I32, cols.shape, 1)
        rank_col = jnp.sum(jnp.where(lane == e, cols, 0.0), axis=1, keepdims=True)
        gate_col = jnp.sum(jnp.where(lane == e + N_EXPERTS, cols, 0.0), axis=1, keepdims=True)
        first_rank = ws_ref[base + i] - region_ref[e]
        want = (lax.broadcasted_iota(I32, (tb, WINDOW_ROWS), 1) + first_rank).astype(F32)
        onehot = jnp.where(want == rank_col, 1.0, 0.0).astype(BF16)
        picked = jnp.dot(onehot, ybuf_ref[slot], preferred_element_type=F32)
        acc_ref[...] += gate_col * picked
        return carry

    lax.fori_loop(0, n, window, 0)

    res = _rms(acc_ref[...], gf_ref[...])

    @pl.when(b < prompt_blocks)
    def _():
        op_ref[...] = res

    @pl.when(b >= prompt_blocks)
    def _():
        os_ref[...] = res


def _combine(h, y, cols, plan, g_final, n_prompt):
    t, d = h.shape
    nwin, wstart, wexp, region = plan
    nb = t // ROW_TILE
    npb = n_prompt // ROW_TILE
    return pl.pallas_call(
        functools.partial(_combine_body, prompt_blocks=npb, max_windows=wstart.shape[0] // nb),
        grid_spec=pltpu.PrefetchScalarGridSpec(
            num_scalar_prefetch=4,
            grid=(nb,),
            in_specs=[pl.BlockSpec((ROW_TILE, cols.shape[1]), lambda b, *_: (b, 0)),
                      pl.BlockSpec(memory_space=pl.ANY),
                      pl.BlockSpec((ROW_TILE, d), lambda b, *_: (b, 0)),
                      pl.BlockSpec((1, d), lambda b, *_: (0, 0))],
            out_specs=[pl.BlockSpec((ROW_TILE, d), lambda b, *_: (jnp.minimum(b, npb - 1), 0)),
                       pl.BlockSpec((ROW_TILE, d), lambda b, *_: (jnp.maximum(b - npb, 0), 0))],
            scratch_shapes=[pltpu.VMEM((ROW_TILE, d), F32),
                            pltpu.VMEM((2, WINDOW_ROWS, d), BF16),
                            pltpu.SemaphoreType.DMA((2,))]),
        out_shape=[jax.ShapeDtypeStruct((n_prompt, d), F32),
                   jax.ShapeDtypeStruct((t - n_prompt, d), F32)],
        compiler_params=_params(("arbitrary",)),
        name="moe_combine",
    )(nwin, wstart, wexp, region, cols, y, h, g_final[None, :])


def _owner(cum, q):
    return jnp.sum(cum[None, :] <= q[:, None], axis=1).astype(I32)


def _moe_plan(cin, cnt, n_tok):
    nb = n_tok // ROW_TILE
    n_slots = n_tok * TOP_K
    s_max = n_slots // MOE_TILE + N_EXPERTS
    counts = cnt[:, 0]
    cblk = jnp.concatenate([cin[:, :, 0].T, counts[:, None]], axis=1)
    ntile = (counts + MOE_TILE - 1) // MOE_TILE
    tcum = jnp.cumsum(ntile)
    tstart = tcum - ntile
    total = tcum[-1]
    s_idx = jnp.arange(s_max, dtype=I32)
    tile_ok = s_idx < total
    tile_e = _owner(tcum, jnp.minimum(s_idx, total - 1))
    tile_k = s_idx - tstart[tile_e]
    tile_rows = jnp.where(tile_ok, jnp.clip(counts[tile_e] - tile_k * MOE_TILE, 0, MOE_TILE), 0)
    tile_nsub = ((tile_rows + SUB_ROWS - 1) // SUB_ROWS).astype(I32)
    region = (tstart * MOE_TILE).astype(I32)

    rel = tile_k * MOE_TILE
    cb = cblk[tile_e]
    b_lo = jnp.clip(jnp.sum(cb[:, 1:] <= rel[:, None], axis=1), 0, nb - 1)
    b_hi = jnp.sum(cb[:, :-1] < (rel + tile_rows)[:, None], axis=1) - 1
    b_hi = jnp.clip(jnp.maximum(b_hi, b_lo), 0, nb - 1)
    fplan = (tile_e, tile_nsub, tile_k.astype(I32), b_lo.astype(I32), b_hi.astype(I32),
             cblk.astype(I32))

    span = (cblk[:, 1:] - cblk[:, :-1]).T
    start = (region[:, None] + cblk[:, :-1]).T
    first = (start // BF16_SUBLANES) * BF16_SUBLANES
    nwin = jnp.where(span > 0, (start + span - first + WINDOW_ROWS - 1) // WINDOW_ROWS, 0)
    wcum = jnp.cumsum(nwin, axis=1)
    wfirst = wcum - nwin
    max_windows = N_EXPERTS * (ROW_TILE // WINDOW_ROWS + 1)
    q = jnp.arange(max_windows, dtype=I32)[None, :]
    q = jnp.minimum(q, wcum[:, -1:] - 1)
    wexp = jnp.sum(wcum[:, None, :] <= q[:, :, None], axis=2).astype(I32)
    k = q - jnp.take_along_axis(wfirst, wexp, axis=1)
    wstart = jnp.take_along_axis(first, wexp, axis=1) + k * WINDOW_ROWS
    cplan = (wcum[:, -1].astype(I32), wstart.reshape(-1).astype(I32), wexp.reshape(-1), region)
    return fplan, cplan


def kernel(x_prompt, x_sample, state_pool, g_mix, w_in, g_v, w_pool, pool_scale, w_s, b_s, w_out,
           g_ffn, dense_w_gate, dense_w_up, dense_w_down, w_router, moe_w_gate, moe_w_up,
           moe_w_down, g_final):
    batch, seq, d = x_prompt.shape
    dec_batch, dec_seq, _ = x_sample.shape
    depth = w_in.shape[0]
    n_p = batch * seq
    n_s = dec_batch * dec_seq
    n_tok = n_p + n_s
    assert dec_seq == 8 and depth == 2 and n_p % ROW_TILE == 0 and n_s % ROW_TILE == 0

    addends = ((x_prompt.reshape(n_p, d), x_sample.reshape(n_s, d)),)
    dn = w_in.shape[2] // 3
    hd = dn // N_HEADS
    per_chunk = CHUNK // dec_seq
    pools_p, pools_s, vs_s = [], [], []
    out = None
    for i in range(depth):
        p, u, v = _inproj(i, addends, g_mix, w_in, g_v)
        bias_p = jnp.repeat(b_s[i].T, hd, axis=1)
        ws_s = w_s[i][:, :dec_seq, :dec_seq]
        eye = jnp.eye(per_chunk, dtype=F32)
        wm_s = (eye[None, :, None, :, None] * ws_s[:, None, :, None, :]).reshape(N_HEADS, CHUNK, CHUNK)
        bias_s = jnp.tile(jnp.repeat(b_s[i][:, :dec_seq].T, hd, axis=1), (per_chunk, 1))
        mix_p = _mixer(i, p, u, v, 0, n_p, seq, 0, w_s[i], bias_p, w_pool, pool_scale)
        mix_s = _mixer(i, p, u, v, n_p, n_s, dec_seq, PAST_LEN, wm_s, bias_s, w_pool, pool_scale,
                       state=state_pool)
        h = _outproj(i, (mix_p, mix_s), w_out, addends)

        pools_p.append(jnp.stack([p[(b + 1) * seq - POOL_BUF:(b + 1) * seq] for b in range(batch)]))
        pools_s.append(jnp.concatenate(
            [state_pool[i, :, dec_seq:], p[n_p:].reshape(dec_batch, dec_seq, dn)], axis=1))
        vs_s.append(v[n_p:].reshape(dec_batch, dec_seq, dn))

        j = i // 2
        if i % 2 == 0:
            xn = _norm(i, h, g_ffn)
            y = _dense_ffn(j, xn, dense_w_gate, dense_w_up, dense_w_down)
            addends = ((h,), (y,))
        else:
            xn, rank, cols, cin, cnt = _router(i, h, g_ffn, w_router[j].T)
            fplan, cplan = _moe_plan(cin, cnt, n_tok)
            y = _moe_ffn(j, xn, rank, fplan, moe_w_gate, moe_w_up, moe_w_down)
            out = _combine(h, y, cols, cplan, g_final, n_p)
            addends = (tuple(out),)

    y_prompt = out[0].reshape(batch, seq, d)
    y_sample = out[1].reshape(dec_batch, dec_seq, d)
    return (y_prompt, y_sample, jnp.stack(pools_p), jnp.stack(pools_s), jnp.stack(vs_s))
```

```python
import functools

import jax
import jax.numpy as jnp
from jax import lax
from jax.experimental import pallas as pl
from jax.experimental.pallas import tpu as pltpu

F32 = jnp.float32
BF16 = jnp.bfloat16
I32 = jnp.int32

EPS = 1e-6
POOL_WINDOWS = (2, 4, 8, 16)
POOL_BUF = max(POOL_WINDOWS) - 1
HALO = 16
CHUNK = 128
N_HEADS = 8
N_EXPERTS = 8
TOP_K = 2
PAST_LEN = 16384

V7X_VMEM_LIMIT = 56 * 1024 * 1024
V7X_MXU_COLS = 256

ROW_TILE = 512
FF_TILE = 256
SUB_ROWS = 256
MOE_TILE = 2560
WINDOW_ROWS = 256
BF16_SUBLANES = 16
DMA_SLOTS = 4
DENSE_TILE = 2304


def _params(sem):
    return pltpu.CompilerParams(dimension_semantics=sem, vmem_limit_bytes=V7X_VMEM_LIMIT)


def _rms(x, g):
    return x * lax.rsqrt(jnp.mean(x * x, axis=-1, keepdims=True) + EPS) * g


def _gelu(x):
    return 0.5 * x * (1.0 + lax.erf(x * (2.0 ** -0.5)))


def _stack_offsets(stack):
    offs, o = [], 0
    for a in stack:
        offs.append(o)
        o += a.shape[0] // ROW_TILE
    return tuple(offs), o


def _stack_specs(stack, cols, m_axis, col_fn=None):
    offs, _ = _stack_offsets(stack)
    specs = []
    for a, off in zip(stack, offs):
        nblk = a.shape[0] // ROW_TILE

        def imap(*idx, off=off, nblk=nblk):
            col = 0 if col_fn is None else col_fn(*idx)
            return (jnp.clip(idx[m_axis] - off, 0, nblk - 1), col)

        specs.append(pl.BlockSpec((ROW_TILE, cols), imap))
    return specs


def _stack_read(refs, offs, m):
    v = refs[0][...]
    for ref, off in zip(refs[1:], offs[1:]):
        v = jnp.where(m >= off, ref[...], v)
    return v


def _sum_read(refs, layout, m):
    total, k = None, 0
    for offs in layout:
        part = _stack_read(refs[k:k + len(offs)], offs, m)
        k += len(offs)
        total = part if total is None else total + part
    return total


def _inproj_body(*refs, layout, kind):
    n_x = sum(len(offs) for offs in layout)
    g_ref, w_ref, gv_ref, o_ref, wbf_ref = refs[n_x:]
    m = pl.program_id(0)

    @pl.when(m == 0)
    def _():
        wbf_ref[...] = w_ref[...].astype(BF16)

    x = _sum_read(refs[:n_x], layout, m)
    rowscale = lax.rsqrt(jnp.mean(x * x, axis=-1, keepdims=True) + EPS)
    xg = (x * g_ref[...]).astype(BF16)
    dn = o_ref.shape[1]
    ss = None
    for c0 in range(0, dn, V7X_MXU_COLS):
        acc = jnp.dot(xg, wbf_ref[:, c0:c0 + V7X_MXU_COLS], preferred_element_type=F32) * rowscale
        if kind != "pool":
            acc = _gelu(acc)
        if kind == "v":
            part = jnp.sum(acc * acc, axis=1, keepdims=True)
            ss = part if ss is None else ss + part
        o_ref[:, c0:c0 + V7X_MXU_COLS] = acc
    if kind == "v":
        o_ref[...] = o_ref[...] * lax.rsqrt(ss / dn + EPS) * gv_ref[...]


def _inproj(layer, addends, g_mix, w_in, g_v):
    d = addends[0][0].shape[1]
    dn = w_in.shape[2] // 3
    layout = tuple(_stack_offsets(st)[0] for st in addends)
    n_blk = _stack_offsets(addends[0])[1]
    x_specs = [sp for st in addends for sp in _stack_specs(st, d, 0)]
    outs = []
    for k, kind in enumerate(("pool", "u", "v")):
        outs.append(pl.pallas_call(
            functools.partial(_inproj_body, layout=layout, kind=kind),
            grid=(n_blk,),
            in_specs=x_specs + [
                pl.BlockSpec((None, 1, d), lambda m: (layer, 0, 0)),
                pl.BlockSpec((None, d, dn), lambda m, k=k: (layer, 0, k), pipeline_mode=pl.Buffered(1)),
                pl.BlockSpec((None, 1, dn), lambda m: (layer, 0, 0)),
            ],
            out_specs=pl.BlockSpec((ROW_TILE, dn), lambda m: (m, 0)),
            out_shape=jax.ShapeDtypeStruct((n_blk * ROW_TILE, dn), F32),
            scratch_shapes=[pltpu.VMEM((d, dn), BF16)],
            compiler_params=_params(("arbitrary",)),
            name=f"inproj{layer}_{kind}",
        )(*[a for st in addends for a in st], g_mix[:, None, :], w_in, g_v[:, None, :]))
    return outs


def _mixer_body(*refs, rows, blocks_per_seq, start_pos, sample):
    p_ref, hist_ref, u_ref, v_ref, wm_ref, bias_ref, wp_ref, ps_ref, o_ref, full_ref = refs
    d_pool = p_ref.shape[1]
    gd = d_pool // len(POOL_WINDOWS)
    hd = u_ref.shape[1] // N_HEADS
    blk = pl.program_id(0) % blocks_per_seq
    p = p_ref[...]

    if sample:
        seqs = rows // 8
        full_ref[:, 1:HALO, :] = hist_ref[...]
        full_ref[:, HALO:HALO + 8, :] = p.reshape(seqs, 8, d_pool)
    else:
        full_ref[0:HALO, :] = jnp.where(blk == 0, 0.0, hist_ref[...])
        full_ref[HALO:HALO + rows, :] = p
    row_id = lax.broadcasted_iota(I32, (rows, 1), 0)
    pos = start_pos + (row_id % 8 if sample else blk * rows + row_id)
    for gi, w in enumerate(POOL_WINDOWS):
        c0 = gi * gd
        s = None
        for j in range(w):
            if sample:
                term = full_ref[:, HALO - j:HALO - j + 8, c0:c0 + gd].reshape(rows, gd)
            else:
                term = full_ref[HALO - j:HALO - j + rows, c0:c0 + gd]
            s = term if s is None else s + term
        cnt = jnp.minimum(pos + 1, w).astype(F32)
        r = s / cnt - p[:, c0:c0 + gd]
        y = jnp.dot(r.astype(BF16), wp_ref[gi].astype(BF16), preferred_element_type=F32)
        o_ref[:, c0:c0 + gd] = (y * ps_ref[:, c0:c0 + gd]).astype(BF16)

    row = lax.broadcasted_iota(I32, (CHUNK, CHUNK), 0)
    col = lax.broadcasted_iota(I32, (CHUNK, CHUNK), 1)
    for h in range(N_HEADS):
        c0 = h * hd
        wm = jnp.where(row >= col, wm_ref[h], 0.0).astype(BF16)
        for c in range(rows // CHUNK):
            r0 = c * CHUNK
            z = jnp.dot(wm, v_ref[r0:r0 + CHUNK, c0:c0 + hd].astype(BF16),
                        preferred_element_type=F32) + bias_ref[:, c0:c0 + hd]
            o_ref[r0:r0 + CHUNK, d_pool + c0:d_pool + c0 + hd] = (
                u_ref[r0:r0 + CHUNK, c0:c0 + hd] * z).astype(BF16)


def _mixer(layer, p, u, v, row0, n_rows, seq_len, start_pos, wm, bias, w_pool, pool_scale,
           state=None):
    dn = p.shape[1]
    sample = state is not None
    rows = ROW_TILE
    blocks_per_seq = 1 if sample else seq_len // rows
    b0 = row0 // rows
    blk_spec = pl.BlockSpec((rows, dn), lambda i: (b0 + i, 0))
    if sample:
        hist_spec = pl.BlockSpec((None, rows // 8, POOL_BUF, dn), lambda i: (layer, i, 0, 0))
        hist = state
        scratch = pltpu.VMEM((rows // 8, HALO + 8, dn), F32)
    else:
        per = rows // HALO
        hist_spec = pl.BlockSpec((HALO, dn), lambda i: (jnp.maximum((b0 + i) * per - 1, 0), 0))
        hist = p
        scratch = pltpu.VMEM((HALO + rows, dn), F32)
    return pl.pallas_call(
        functools.partial(_mixer_body, rows=rows, blocks_per_seq=blocks_per_seq,
                          start_pos=start_pos, sample=sample),
        grid=(n_rows // rows,),
        in_specs=[
            blk_spec, hist_spec, blk_spec, blk_spec,
            pl.BlockSpec((N_HEADS, CHUNK, CHUNK), lambda i: (0, 0, 0)),
            pl.BlockSpec((CHUNK, dn), lambda i: (0, 0)),
            pl.BlockSpec((None,) + w_pool.shape[1:], lambda i: (layer, 0, 0, 0)),
            pl.BlockSpec((None, 1, dn), lambda i: (layer, 0, 0)),
        ],
        out_specs=pl.BlockSpec((rows, 2 * dn), lambda i: (i, 0)),
        out_shape=jax.ShapeDtypeStruct((n_rows, 2 * dn), BF16),
        scratch_shapes=[scratch],
        compiler_params=_params(("arbitrary",)),
        name=f"mixer{layer}_{'sample' if sample else 'prompt'}",
    )(p, hist, u, v, wm, bias, w_pool, pool_scale[:, None, :])


def _outproj_body(*refs, mix_offs, layout):
    n_mix = len(mix_offs)
    n_h = sum(len(offs) for offs in layout)
    w_ref = refs[n_mix]
    h_refs = refs[n_mix + 1:n_mix + 1 + n_h]
    o_ref, wbf_ref = refs[n_mix + 1 + n_h:]
    m = pl.program_id(1)

    @pl.when(m == 0)
    def _():
        wbf_ref[...] = w_ref[...].astype(BF16)

    a = _stack_read(refs[:n_mix], mix_offs, m)
    o_ref[...] = _sum_read(h_refs, layout, m) + jnp.dot(a, wbf_ref[...], preferred_element_type=F32)


def _outproj(layer, mix, w_out, addends):
    k = mix[0].shape[1]
    d = w_out.shape[2]
    dn = d // 2
    mix_offs, n_blk = _stack_offsets(mix)
    layout = tuple(_stack_offsets(st)[0] for st in addends)
    h_specs = [sp for st in addends for sp in _stack_specs(st, dn, 1, col_fn=lambda n, m: n)]
    return pl.pallas_call(
        functools.partial(_outproj_body, mix_offs=mix_offs, layout=layout),
        grid=(2, n_blk),
        in_specs=_stack_specs(mix, k, 1) + [
            pl.BlockSpec((None, k, dn), lambda n, m: (layer, 0, n)),
        ] + h_specs,
        out_specs=pl.BlockSpec((ROW_TILE, dn), lambda n, m: (m, n)),
        out_shape=jax.ShapeDtypeStruct((n_blk * ROW_TILE, d), F32),
        scratch_shapes=[pltpu.VMEM((k, dn), BF16)],
        compiler_params=_params(("arbitrary", "arbitrary")),
        name=f"outproj{layer}",
    )(*mix, w_out, *[a for st in addends for a in st])


def _norm_body(h_ref, g_ref, o_ref):
    o_ref[...] = _rms(h_ref[...], g_ref[...]).astype(BF16)


def _norm(layer, h, g):
    t, d = h.shape
    return pl.pallas_call(
        _norm_body,
        grid=(t // ROW_TILE,),
        in_specs=[pl.BlockSpec((ROW_TILE, d), lambda i: (i, 0)),
                  pl.BlockSpec((None, 1, d), lambda i: (layer, 0, 0))],
        out_specs=pl.BlockSpec((ROW_TILE, d), lambda i: (i, 0)),
        out_shape=jax.ShapeDtypeStruct((t, d), BF16),
        compiler_params=_params(("arbitrary",)),
        name=f"norm{layer}",
    )(h, g[:, None, :])


def _router_body(h_ref, g_ref, wr_ref, xn_ref, rank_ref, col_ref, cin_ref, cnt_ref, carry_ref):
    tb = h_ref.shape[0]

    @pl.when(pl.program_id(0) == 0)
    def _():
        carry_ref[...] = jnp.zeros_like(carry_ref)

    xn = _rms(h_ref[...], g_ref[...])
    xn_ref[...] = xn.astype(BF16)
    logits = lax.dot_general(wr_ref[...], xn, (((1,), (1,)), ((), ())),
                             precision=lax.Precision.HIGHEST, preferred_element_type=F32)
    eidx = lax.broadcasted_iota(I32, logits.shape, 0)
    m1 = jnp.max(logits, axis=0, keepdims=True)
    i1 = jnp.min(jnp.where(logits == m1, eidx, N_EXPERTS), axis=0, keepdims=True)
    sel1 = eidx == i1
    rest = jnp.where(sel1, -jnp.inf, logits)
    m2 = jnp.max(rest, axis=0, keepdims=True)
    i2 = jnp.min(jnp.where(rest == m2, eidx, N_EXPERTS), axis=0, keepdims=True)
    sel2 = eidx == i2
    e2 = jnp.exp(m2 - m1)
    g1 = 1.0 / (1.0 + e2)
    g2 = e2 / (1.0 + e2)
    gate = jnp.where(sel1, g1, jnp.where(sel2, g2, 0.0))
    routed = jnp.where(sel1, 1.0, jnp.where(sel2, 1.0, 0.0))
    before = lax.broadcasted_iota(I32, (tb, tb), 0) < lax.broadcasted_iota(I32, (tb, tb), 1)
    upper = jnp.where(before, 1.0, 0.0).astype(BF16)
    excl = jnp.dot(routed.astype(BF16), upper, preferred_element_type=F32)
    carry = carry_ref[...]
    rank = jnp.where(routed > 0.0, excl + carry[:, 0:1], -1.0)
    rank_ref[...] = rank.astype(I32)
    same = lax.broadcasted_iota(I32, (tb, tb), 0) == lax.broadcasted_iota(I32, (tb, tb), 1)
    col_ref[...] = lax.dot_general(jnp.where(same, 1.0, 0.0), jnp.concatenate([rank, gate], axis=0),
                                   (((1,), (1,)), ((), ())),
                                   precision=lax.Precision.HIGHEST, preferred_element_type=F32)
    cin_ref[...] = carry.astype(I32)
    new_carry = carry + jnp.sum(routed, axis=1, keepdims=True)
    carry_ref[...] = new_carry
    cnt_ref[...] = new_carry.astype(I32)


def _router(layer, h, g, w_router_t):
    t, d = h.shape
    nb = t // ROW_TILE
    e = w_router_t.shape[0]
    return pl.pallas_call(
        _router_body,
        grid=(nb,),
        in_specs=[pl.BlockSpec((ROW_TILE, d), lambda i: (i, 0)),
                  pl.BlockSpec((None, 1, d), lambda i: (layer, 0, 0)),
                  pl.BlockSpec((e, d), lambda i: (0, 0))],
        out_specs=[pl.BlockSpec((ROW_TILE, d), lambda i: (i, 0)),
                   pl.BlockSpec((e, ROW_TILE), lambda i: (0, i)),
                   pl.BlockSpec((ROW_TILE, 2 * e), lambda i: (i, 0)),
                   pl.BlockSpec((None, e, 128), lambda i: (i, 0, 0)),
                   pl.BlockSpec((e, 128), lambda i: (0, 0))],
        out_shape=[jax.ShapeDtypeStruct((t, d), BF16),
                   jax.ShapeDtypeStruct((e, t), I32),
                   jax.ShapeDtypeStruct((t, 2 * e), F32),
                   jax.ShapeDtypeStruct((nb, e, 128), I32),
                   jax.ShapeDtypeStruct((e, 128), I32)],
        scratch_shapes=[pltpu.VMEM((e, 128), F32)],
        compiler_params=_params(("arbitrary",)),
        name=f"router{layer}",
    )(h, g[:, None, :], w_router_t)


def _swiglu_blocks(x_ref, wg_ref, wu_ref, wd_ref, acc_ref, nsub):
    def rows_block(first_sub, n_sub):
        n_rows = n_sub * SUB_ROWS
        r0 = first_sub * SUB_ROWS
        if not isinstance(r0, int):
            r0 = pl.multiple_of(r0, SUB_ROWS)
        x = x_ref[pl.ds(r0, n_rows), :]
        g = jnp.dot(x, wg_ref[...].astype(BF16), preferred_element_type=F32)
        u = jnp.dot(x, wu_ref[...].astype(BF16), preferred_element_type=F32)
        a = (g * jax.nn.sigmoid(g) * u).astype(BF16)
        acc_ref[pl.ds(r0, n_rows), :] += jnp.dot(a, wd_ref[...].astype(BF16),
                                                 preferred_element_type=F32)

    if isinstance(nsub, int):
        for m in range(nsub // 4):
            rows_block(m * 4, 4)
        if nsub % 4 >= 2:
            rows_block((nsub // 4) * 4, 2)
        if nsub % 2 == 1:
            rows_block(nsub - 1, 1)
        return

    def quad(m, carry):
        rows_block(m * 4, 4)
        return carry

    lax.fori_loop(0, nsub // 4, quad, 0)

    @pl.when(nsub % 4 >= 2)
    def _():
        rows_block((nsub // 4) * 4, 2)

    @pl.when(nsub % 2 == 1)
    def _():
        rows_block(nsub - 1, 1)


def _dense_ffn_body(x_ref, wg_ref, wu_ref, wd_ref, o_ref):
    @pl.when(pl.program_id(1) == 0)
    def _():
        o_ref[...] = jnp.zeros_like(o_ref)

    _swiglu_blocks(x_ref, wg_ref, wu_ref, wd_ref, o_ref, x_ref.shape[0] // SUB_ROWS)


def _dense_ffn(layer, x, wg, wu, wd):
    rows, d = x.shape
    nf = wg.shape[2] // FF_TILE
    return pl.pallas_call(
        _dense_ffn_body,
        grid=(rows // DENSE_TILE, nf),
        in_specs=[pl.BlockSpec((DENSE_TILE, d), lambda s, f: (s, 0), pipeline_mode=pl.Buffered(1)),
                  pl.BlockSpec((None, d, FF_TILE), lambda s, f: (layer, 0, f)),
                  pl.BlockSpec((None, d, FF_TILE), lambda s, f: (layer, 0, f)),
                  pl.BlockSpec((None, FF_TILE, d), lambda s, f: (layer, f, 0))],
        out_specs=pl.BlockSpec((DENSE_TILE, d), lambda s, f: (s, 0), pipeline_mode=pl.Buffered(1)),
        out_shape=jax.ShapeDtypeStruct((rows, d), F32),
        compiler_params=_params(("arbitrary", "arbitrary")),
        name=f"ffn_dense{layer}",
    )(x, wg, wu, wd)


def _moe_ffn_body(te_ref, ns_ref, tk_ref, blo_ref, bhi_ref, cb_ref,
                  rank_ref, xn_hbm, wg_ref, wu_ref, wd_ref, y_hbm,
                  xs_ref, acc_ref, xbuf_ref, in_sem, out_sem, *, n_blocks):
    s = pl.program_id(0)
    f = pl.program_id(1)
    nsub = ns_ref[s]
    e = te_ref[s]
    tile_rows = xs_ref.shape[0]
    tb = xbuf_ref.shape[1]
    rel0 = tk_ref[s] * tile_rows

    def fetch(b, slot):
        src = xn_hbm.at[pl.ds(pl.multiple_of(b * tb, tb), tb), :]
        return pltpu.make_async_copy(src, xbuf_ref.at[slot], in_sem.at[slot])

    @pl.when((f == 0) & (nsub > 0))
    def _():
        blo = blo_ref[s]
        bhi = bhi_ref[s]
        for ahead in range(DMA_SLOTS - 1):
            @pl.when(blo + ahead <= bhi)
            def _():
                fetch(blo + ahead, ahead).start()

        acc_ref[...] = jnp.zeros_like(acc_ref)
        xs_ref[...] = jnp.zeros_like(xs_ref)

        def token_block(b, carry):
            slot = (b - blo) % DMA_SLOTS
            fetch(b, slot).wait()
            nxt = b + (DMA_SLOTS - 1)

            @pl.when(nxt <= bhi)
            def _():
                fetch(nxt, (nxt - blo) % DMA_SLOTS).start()

            c0 = cb_ref[e * (n_blocks + 1) + b]
            c1 = cb_ref[e * (n_blocks + 1) + b + 1]
            j_lo = jnp.clip((c0 - rel0) // SUB_ROWS, 0, nsub - 1)
            j_hi = jnp.clip((c1 - 1 - rel0) // SUB_ROWS, 0, nsub - 1)
            rank_row = rank_ref[pl.ds(e, 1), pl.ds(pl.multiple_of(b * tb, tb), tb)]
            xb = xbuf_ref.at[slot]

            def sub_block(j, c):
                r0 = pl.multiple_of(j * SUB_ROWS, SUB_ROWS)
                want = lax.broadcasted_iota(I32, (SUB_ROWS, tb), 0) + (rel0 + r0)
                onehot = jnp.where(want == rank_row, 1.0, 0.0).astype(BF16)
                xs_ref[pl.ds(r0, SUB_ROWS), :] += jnp.dot(
                    onehot, xb[...], preferred_element_type=F32).astype(BF16)
                return c

            lax.fori_loop(j_lo, j_hi + 1, sub_block, 0)
            return carry

        lax.fori_loop(blo, bhi + 1, token_block, 0)

    _swiglu_blocks(xs_ref, wg_ref, wu_ref, wd_ref, acc_ref, nsub)

    @pl.when(f == pl.num_programs(1) - 1)
    def _():
        @pl.when(nsub > 0)
        def _():
            xs_ref[...] = acc_ref[...].astype(BF16)

        @pl.when(nsub == 0)
        def _():
            xs_ref[...] = jnp.zeros_like(xs_ref)

        dst = y_hbm.at[pl.ds(pl.multiple_of(s * tile_rows, tile_rows), tile_rows), :]
        out = pltpu.make_async_copy(xs_ref, dst, out_sem.at[0])
        out.start()
        out.wait()

        @pl.when(s == pl.num_programs(0) - 1)
        def _():
            xs_ref[0:WINDOW_ROWS, :] = jnp.zeros((WINDOW_ROWS, xs_ref.shape[1]), BF16)
            end = y_hbm.shape[0] - WINDOW_ROWS
            tail = pltpu.make_async_copy(xs_ref.at[pl.ds(0, WINDOW_ROWS), :],
                                         y_hbm.at[pl.ds(end, WINDOW_ROWS), :], out_sem.at[0])
            tail.start()
            tail.wait()


def _moe_ffn(layer, xn, rank, plan, wg, wu, wd):
    tile_e, tile_nsub, tile_k, tile_blo, tile_bhi, cblk = plan
    t, d = xn.shape
    ff = wg.shape[3]
    nf = ff // FF_TILE
    n_tiles = tile_e.shape[0]
    nb = t // ROW_TILE

    def w_col(s, f, te, ns, *_):
        return (layer, te[s], 0, jnp.where(ns[s] > 0, f, nf - 1))

    def w_row(s, f, te, ns, *_):
        return (layer, te[s], jnp.where(ns[s] > 0, f, nf - 1), 0)

    return pl.pallas_call(
        functools.partial(_moe_ffn_body, n_blocks=nb),
        grid_spec=pltpu.PrefetchScalarGridSpec(
            num_scalar_prefetch=6,
            grid=(n_tiles, nf),
            in_specs=[pl.BlockSpec(rank.shape, lambda s, f, *_: (0, 0), pipeline_mode=pl.Buffered(1)),
                      pl.BlockSpec(memory_space=pl.ANY),
                      pl.BlockSpec((None, None, d, FF_TILE), w_col),
                      pl.BlockSpec((None, None, d, FF_TILE), w_col),
                      pl.BlockSpec((None, None, FF_TILE, d), w_row)],
            out_specs=pl.BlockSpec(memory_space=pl.ANY),
            scratch_shapes=[pltpu.VMEM((MOE_TILE, d), BF16),
                            pltpu.VMEM((MOE_TILE, d), F32),
                            pltpu.VMEM((DMA_SLOTS, ROW_TILE, d), BF16),
                            pltpu.SemaphoreType.DMA((DMA_SLOTS,)),
                            pltpu.SemaphoreType.DMA((1,))]),
        out_shape=jax.ShapeDtypeStruct((n_tiles * MOE_TILE + WINDOW_ROWS, d), BF16),
        compiler_params=_params(("arbitrary", "arbitrary")),
        name=f"ffn_moe{layer}",
    )(tile_e, tile_nsub, tile_k, tile_blo, tile_bhi, cblk.reshape(-1), rank, xn, wg, wu, wd)


def _combine_body(nwin_ref, ws_ref, we_ref, region_ref, col_ref, y_hbm, h_ref, gf_ref,
                  op_ref, os_ref, acc_ref, ybuf_ref, sem, *, prompt_blocks, max_windows):
    b = pl.program_id(0)
    tb = h_ref.shape[0]
    n = nwin_ref[b]
    base = b * max_windows

    def fetch(i, slot):
        start = pl.multiple_of(ws_ref[base + i], BF16_SUBLANES)
        return pltpu.make_async_copy(y_hbm.at[pl.ds(start, WINDOW_ROWS), :], ybuf_ref.at[slot],
                                     sem.at[slot])

    for ahead in range(DMA_SLOTS - 1):
        @pl.when(ahead < n)
        def _():
            fetch(ahead, ahead).start()

    acc_ref[...] = h_ref[...]

    def window(i, carry):
        slot = i % DMA_SLOTS
        fetch(i, slot).wait()
        nxt = i + (DMA_SLOTS - 1)

        @pl.when(nxt < n)
        def _():
            fetch(nxt, nxt % DMA_SLOTS).start()

        e = we_ref[base + i]
        cols = col_ref[...]
        lane = lax.broadcasted_iota(I32, cols.shape, 1)
        rank_col = jnp.sum(jnp.where(lane == e, cols, 0.0), axis=1, keepdims=True)
        gate_col = jnp.sum(jnp.where(lane == e + N_EXPERTS, cols, 0.0), axis=1, keepdims=True)
        first_rank = ws_ref[base + i] - region_ref[e]
        want = (lax.broadcasted_iota(I32, (tb, WINDOW_ROWS), 1) + first_rank).astype(F32)
        onehot = jnp.where(want == rank_col, 1.0, 0.0).astype(BF16)
        picked = jnp.dot(onehot, ybuf_ref[slot], preferred_element_type=F32)
        acc_ref[...] += gate_col * picked
        return carry

    lax.fori_loop(0, n, window, 0)

    res = _rms(acc_ref[...], gf_ref[...])

    @pl.when(b < prompt_blocks)
    def _():
        op_ref[...] = res

    @pl.when(b >= prompt_blocks)
    def _():
        os_ref[...] = res


def _combine(h, y, cols, plan, g_final, n_prompt):
    t, d = h.shape
    nwin, wstart, wexp, region = plan
    nb = t // ROW_TILE
    npb = n_prompt // ROW_TILE
    return pl.pallas_call(
        functools.partial(_combine_body, prompt_blocks=npb, max_windows=wstart.shape[0] // nb),
        grid_spec=pltpu.PrefetchScalarGridSpec(
            num_scalar_prefetch=4,
            grid=(nb,),
            in_specs=[pl.BlockSpec((ROW_TILE, cols.shape[1]), lambda b, *_: (b, 0)),
                      pl.BlockSpec(memory_space=pl.ANY),
                      pl.BlockSpec((ROW_TILE, d), lambda b, *_: (b, 0)),
                      pl.BlockSpec((1, d), lambda b, *_: (0, 0))],
            out_specs=[pl.BlockSpec((ROW_TILE, d), lambda b, *_: (jnp.minimum(b, npb - 1), 0)),
                       pl.BlockSpec((ROW_TILE, d), lambda b, *_: (jnp.maximum(b - npb, 0), 0))],
            scratch_shapes=[pltpu.VMEM((ROW_TILE, d), F32),
                            pltpu.VMEM((DMA_SLOTS, WINDOW_ROWS, d), BF16),
                            pltpu.SemaphoreType.DMA((DMA_SLOTS,))]),
        out_shape=[jax.ShapeDtypeStruct((n_prompt, d), F32),
                   jax.ShapeDtypeStruct((t - n_prompt, d), F32)],
        compiler_params=_params(("arbitrary",)),
        name="moe_combine",
    )(nwin, wstart, wexp, region, cols, y, h, g_final[None, :])


def _owner(cum, q):
    return jnp.sum(cum[None, :] <= q[:, None], axis=1).astype(I32)


def _moe_plan(cin, cnt, n_tok):
    nb = n_tok // ROW_TILE
    n_slots = n_tok * TOP_K
    s_max = n_slots // MOE_TILE + N_EXPERTS
    counts = cnt[:, 0]
    cblk = jnp.concatenate([cin[:, :, 0].T, counts[:, None]], axis=1)
    ntile = (counts + MOE_TILE - 1) // MOE_TILE
    tcum = jnp.cumsum(ntile)
    tstart = tcum - ntile
    total = tcum[-1]
    s_idx = jnp.arange(s_max, dtype=I32)
    tile_ok = s_idx < total
    tile_e = _owner(tcum, jnp.minimum(s_idx, total - 1))
    tile_k = s_idx - tstart[tile_e]
    tile_rows = jnp.where(tile_ok, jnp.clip(counts[tile_e] - tile_k * MOE_TILE, 0, MOE_TILE), 0)
    tile_nsub = ((tile_rows + SUB_ROWS - 1) // SUB_ROWS).astype(I32)
    region = (tstart * MOE_TILE).astype(I32)

    rel = tile_k * MOE_TILE
    cb = cblk[tile_e]
    b_lo = jnp.clip(jnp.sum(cb[:, 1:] <= rel[:, None], axis=1), 0, nb - 1)
    b_hi = jnp.sum(cb[:, :-1] < (rel + tile_rows)[:, None], axis=1) - 1
    b_hi = jnp.clip(jnp.maximum(b_hi, b_lo), 0, nb - 1)
    fplan = (tile_e, tile_nsub, tile_k.astype(I32), b_lo.astype(I32), b_hi.astype(I32),
             cblk.astype(I32))

    span = (cblk[:, 1:] - cblk[:, :-1]).T
    start = (region[:, None] + cblk[:, :-1]).T
    first = (start // BF16_SUBLANES) * BF16_SUBLANES
    nwin = jnp.where(span > 0, (start + span - first + WINDOW_ROWS - 1) // WINDOW_ROWS, 0)
    wcum = jnp.cumsum(nwin, axis=1)
    wfirst = wcum - nwin
    max_windows = N_EXPERTS * (ROW_TILE // WINDOW_ROWS + 1)
    q = jnp.arange(max_windows, dtype=I32)[None, :]
    q = jnp.minimum(q, wcum[:, -1:] - 1)
    wexp = jnp.sum(wcum[:, None, :] <= q[:, :, None], axis=2).astype(I32)
    k = q - jnp.take_along_axis(wfirst, wexp, axis=1)
    wstart = jnp.take_along_axis(first, wexp, axis=1) + k * WINDOW_ROWS
    cplan = (wcum[:, -1].astype(I32), wstart.reshape(-1).astype(I32), wexp.reshape(-1), region)
    return fplan, cplan


def kernel(x_prompt, x_sample, state_pool, g_mix, w_in, g_v, w_pool, pool_scale, w_s, b_s, w_out,
           g_ffn, dense_w_gate, dense_w_up, dense_w_down, w_router, moe_w_gate, moe_w_up,
           moe_w_down, g_final):
    batch, seq, d = x_prompt.shape
    dec_batch, dec_seq, _ = x_sample.shape
    depth = w_in.shape[0]
    n_p = batch * seq
    n_s = dec_batch * dec_seq
    n_tok = n_p + n_s
    assert dec_seq == 8 and depth == 2 and n_p % ROW_TILE == 0 and n_s % ROW_TILE == 0

    addends = ((x_prompt.reshape(n_p, d), x_sample.reshape(n_s, d)),)
    dn = w_in.shape[2] // 3
    hd = dn // N_HEADS
    per_chunk = CHUNK // dec_seq
    pools_p, pools_s, vs_s = [], [], []
    out = None
    for i in range(depth):
        p, u, v = _inproj(i, addends, g_mix, w_in, g_v)
        bias_p = jnp.repeat(b_s[i].T, hd, axis=1)
        ws_s = w_s[i][:, :dec_seq, :dec_seq]
        eye = jnp.eye(per_chunk, dtype=F32)
        wm_s = (eye[None, :, None, :, None] * ws_s[:, None, :, None, :]).reshape(N_HEADS, CHUNK, CHUNK)
        bias_s = jnp.tile(jnp.repeat(b_s[i][:, :dec_seq].T, hd, axis=1), (per_chunk, 1))
        mix_p = _mixer(i, p, u, v, 0, n_p, seq, 0, w_s[i], bias_p, w_pool, pool_scale)
        mix_s = _mixer(i, p, u, v, n_p, n_s, dec_seq, PAST_LEN, wm_s, bias_s, w_pool, pool_scale,
                       state=state_pool)
        h = _outproj(i, (mix_p, mix_s), w_out, addends)

        pools_p.append(jnp.stack([p[(b + 1) * seq - POOL_BUF:(b + 1) * seq] for b in range(batch)]))
        pools_s.append(jnp.concatenate(
            [state_pool[i, :, dec_seq:], p[n_p:].reshape(dec_batch, dec_seq, dn)], axis=1))
        vs_s.append(v[n_p:].reshape(dec_batch, dec_seq, dn))

        j = i // 2
        if i % 2 == 0:
            xn = _norm(i, h, g_ffn)
            y = _dense_ffn(j, xn, dense_w_gate, dense_w_up, dense_w_down)
            addends = ((h,), (y,))
        else:
            xn, rank, cols, cin, cnt = _router(i, h, g_ffn, w_router[j].T)
            fplan, cplan = _moe_plan(cin, cnt, n_tok)
            y = _moe_ffn(j, xn, rank, fplan, moe_w_gate, moe_w_up, moe_w_down)
            out = _combine(h, y, cols, cplan, g_final, n_p)
            addends = (tuple(out),)

    y_prompt = out[0].reshape(batch, seq, d)
    y_sample = out[1].reshape(dec_batch, dec_seq, d)
    return (y_prompt, y_sample, jnp.stack(pools_p), jnp.stack(pools_s), jnp.stack(vs_s))
```

```python
import functools

import jax
import jax.numpy as jnp
from jax import lax
from jax.experimental import pallas as pl
from jax.experimental.pallas import tpu as pltpu

F32 = jnp.float32
BF16 = jnp.bfloat16
I32 = jnp.int32

EPS = 1e-6
POOL_WINDOWS = (2, 4, 8, 16)
POOL_BUF = max(POOL_WINDOWS) - 1
HALO = 16
CHUNK = 128
N_HEADS = 8
N_EXPERTS = 8
TOP_K = 2
PAST_LEN = 16384

V7X_VMEM_LIMIT = 56 * 1024 * 1024
V7X_MXU_COLS = 256

ROW_TILE = 512
FF_TILE = 256
SUB_ROWS = 256
MOE_TILE = 2560
WINDOW_ROWS = 256
BF16_SUBLANES = 16
DMA_SLOTS = 4
DENSE_TILE = 2304


def _params(sem):
    return pltpu.CompilerParams(dimension_semantics=sem, vmem_limit_bytes=V7X_VMEM_LIMIT)


def _rms(x, g):
    return x * lax.rsqrt(jnp.mean(x * x, axis=-1, keepdims=True) + EPS) * g


def _gelu(x):
    return 0.5 * x * (1.0 + lax.erf(x * (2.0 ** -0.5)))


def _stack_offsets(stack):
    offs, o = [], 0
    for a in stack:
        offs.append(o)
        o += a.shape[0] // ROW_TILE
    return tuple(offs), o


def _stack_specs(stack, cols, m_axis, col_fn=None):
    offs, _ = _stack_offsets(stack)
    specs = []
    for a, off in zip(stack, offs):
        nblk = a.shape[0] // ROW_TILE

        def imap(*idx, off=off, nblk=nblk):
            col = 0 if col_fn is None else col_fn(*idx)
            return (jnp.clip(idx[m_axis] - off, 0, nblk - 1), col)

        specs.append(pl.BlockSpec((ROW_TILE, cols), imap))
    return specs


def _stack_read(refs, offs, m):
    v = refs[0][...]
    for ref, off in zip(refs[1:], offs[1:]):
        v = jnp.where(m >= off, ref[...], v)
    return v


def _sum_read(refs, layout, m):
    total, k = None, 0
    for offs in layout:
        part = _stack_read(refs[k:k + len(offs)], offs, m)
        k += len(offs)
        total = part if total is None else total + part
    return total


def _inproj_body(*refs, layout):
    n_x = sum(len(offs) for offs in layout)
    g_ref, w_ref, gv_ref, p_ref, u_ref, v_ref = refs[n_x:]
    m = pl.program_id(0)
    x = _sum_read(refs[:n_x], layout, m)
    rowscale = lax.rsqrt(jnp.mean(x * x, axis=-1, keepdims=True) + EPS)
    xg = (x * g_ref[...]).astype(BF16)
    dn = p_ref.shape[1]
    ss = None
    for k, o_ref in enumerate((p_ref, u_ref, v_ref)):
        for c0 in range(0, dn, V7X_MXU_COLS):
            w = w_ref[:, k * dn + c0:k * dn + c0 + V7X_MXU_COLS]
            acc = jnp.dot(xg, w, preferred_element_type=F32) * rowscale
            if o_ref is not p_ref:
                acc = _gelu(acc)
            if o_ref is v_ref:
                part = jnp.sum(acc * acc, axis=1, keepdims=True)
                ss = part if ss is None else ss + part
            o_ref[:, c0:c0 + V7X_MXU_COLS] = acc
    v_ref[...] = v_ref[...] * lax.rsqrt(ss / dn + EPS) * gv_ref[...]


def _inproj(layer, addends, g_mix, w_in_bf16, g_v):
    d = addends[0][0].shape[1]
    dn = w_in_bf16.shape[2] // 3
    layout = tuple(_stack_offsets(st)[0] for st in addends)
    n_blk = _stack_offsets(addends[0])[1]
    x_specs = [sp for st in addends for sp in _stack_specs(st, d, 0)]
    out_spec = pl.BlockSpec((ROW_TILE, dn), lambda m: (m, 0))
    out_shape = jax.ShapeDtypeStruct((n_blk * ROW_TILE, dn), F32)
    return pl.pallas_call(
        functools.partial(_inproj_body, layout=layout),
        grid=(n_blk,),
        in_specs=x_specs + [
            pl.BlockSpec((None, 1, d), lambda m: (layer, 0, 0)),
            pl.BlockSpec((None, d, 3 * dn), lambda m: (layer, 0, 0), pipeline_mode=pl.Buffered(1)),
            pl.BlockSpec((None, 1, dn), lambda m: (layer, 0, 0)),
        ],
        out_specs=[out_spec] * 3,
        out_shape=[out_shape] * 3,
        compiler_params=_params(("arbitrary",)),
        name=f"inproj{layer}",
    )(*[a for st in addends for a in st], g_mix[:, None, :], w_in_bf16, g_v[:, None, :])


def _mixer_body(*refs, rows, blocks_per_seq, start_pos, sample):
    p_ref, hist_ref, u_ref, v_ref, wm_ref, bias_ref, wp_ref, ps_ref, o_ref, full_ref = refs
    d_pool = p_ref.shape[1]
    gd = d_pool // len(POOL_WINDOWS)
    hd = u_ref.shape[1] // N_HEADS
    blk = pl.program_id(0) % blocks_per_seq
    p = p_ref[...]

    if sample:
        seqs = rows // 8
        full_ref[:, 1:HALO, :] = hist_ref[...]
        full_ref[:, HALO:HALO + 8, :] = p.reshape(seqs, 8, d_pool)
    else:
        full_ref[0:HALO, :] = jnp.where(blk == 0, 0.0, hist_ref[...])
        full_ref[HALO:HALO + rows, :] = p
    row_id = lax.broadcasted_iota(I32, (rows, 1), 0)
    pos = start_pos + (row_id % 8 if sample else blk * rows + row_id)
    for gi, w in enumerate(POOL_WINDOWS):
        c0 = gi * gd
        s = None
        for j in range(w):
            if sample:
                term = full_ref[:, HALO - j:HALO - j + 8, c0:c0 + gd].reshape(rows, gd)
            else:
                term = full_ref[HALO - j:HALO - j + rows, c0:c0 + gd]
            s = term if s is None else s + term
        cnt = jnp.minimum(pos + 1, w).astype(F32)
        r = s / cnt - p[:, c0:c0 + gd]
        y = jnp.dot(r.astype(BF16), wp_ref[gi].astype(BF16), preferred_element_type=F32)
        o_ref[:, c0:c0 + gd] = (y * ps_ref[:, c0:c0 + gd]).astype(BF16)

    row = lax.broadcasted_iota(I32, (CHUNK, CHUNK), 0)
    col = lax.broadcasted_iota(I32, (CHUNK, CHUNK), 1)
    for h in range(N_HEADS):
        c0 = h * hd
        wm = jnp.where(row >= col, wm_ref[h], 0.0).astype(BF16)
        for c in range(rows // CHUNK):
            r0 = c * CHUNK
            z = jnp.dot(wm, v_ref[r0:r0 + CHUNK, c0:c0 + hd].astype(BF16),
                        preferred_element_type=F32) + bias_ref[:, c0:c0 + hd]
            o_ref[r0:r0 + CHUNK, d_pool + c0:d_pool + c0 + hd] = (
                u_ref[r0:r0 + CHUNK, c0:c0 + hd] * z).astype(BF16)


def _mixer(layer, p, u, v, row0, n_rows, seq_len, start_pos, wm, bias, w_pool, pool_scale,
           state=None):
    dn = p.shape[1]
    sample = state is not None
    rows = ROW_TILE
    blocks_per_seq = 1 if sample else seq_len // rows
    b0 = row0 // rows
    blk_spec = pl.BlockSpec((rows, dn), lambda i: (b0 + i, 0))
    if sample:
        hist_spec = pl.BlockSpec((None, rows // 8, POOL_BUF, dn), lambda i: (layer, i, 0, 0))
        hist = state
        scratch = pltpu.VMEM((rows // 8, HALO + 8, dn), F32)
    else:
        per = rows // HALO
        hist_spec = pl.BlockSpec((HALO, dn), lambda i: (jnp.maximum((b0 + i) * per - 1, 0), 0))
        hist = p
        scratch = pltpu.VMEM((HALO + rows, dn), F32)
    return pl.pallas_call(
        functools.partial(_mixer_body, rows=rows, blocks_per_seq=blocks_per_seq,
                          start_pos=start_pos, sample=sample),
        grid=(n_rows // rows,),
        in_specs=[
            blk_spec, hist_spec, blk_spec, blk_spec,
            pl.BlockSpec((N_HEADS, CHUNK, CHUNK), lambda i: (0, 0, 0)),
            pl.BlockSpec((CHUNK, dn), lambda i: (0, 0)),
            pl.BlockSpec((None,) + w_pool.shape[1:], lambda i: (layer, 0, 0, 0)),
            pl.BlockSpec((None, 1, dn), lambda i: (layer, 0, 0)),
        ],
        out_specs=pl.BlockSpec((rows, 2 * dn), lambda i: (i, 0)),
        out_shape=jax.ShapeDtypeStruct((n_rows, 2 * dn), BF16),
        scratch_shapes=[scratch],
        compiler_params=_params(("arbitrary",)),
        name=f"mixer{layer}_{'sample' if sample else 'prompt'}",
    )(p, hist, u, v, wm, bias, w_pool, pool_scale[:, None, :])


def _outproj_body(*refs, mix_offs, layout, with_norm):
    n_mix = len(mix_offs)
    n_h = sum(len(offs) for offs in layout)
    w_ref = refs[n_mix]
    h_refs = refs[n_mix + 1:n_mix + 1 + n_h]
    if with_norm:
        g_ref, o_ref, xn_ref = refs[n_mix + 1 + n_h:]
    else:
        o_ref, = refs[n_mix + 1 + n_h:]
    m = pl.program_id(0)
    a = _stack_read(refs[:n_mix], mix_offs, m)
    h = _sum_read(h_refs, layout, m)
    d = o_ref.shape[1]
    ss = None
    for c0 in range(0, d, V7X_MXU_COLS):
        cols = slice(c0, c0 + V7X_MXU_COLS)
        hn = h[:, cols] + jnp.dot(a, w_ref[:, cols], preferred_element_type=F32)
        o_ref[:, cols] = hn
        if with_norm:
            part = jnp.sum(hn * hn, axis=1, keepdims=True)
            ss = part if ss is None else ss + part
    if with_norm:
        xn_ref[...] = (o_ref[...] * lax.rsqrt(ss / d + EPS) * g_ref[...]).astype(BF16)


def _outproj(layer, mix, w_out_bf16, addends, g_next=None):
    k = mix[0].shape[1]
    d = w_out_bf16.shape[2]
    mix_offs, n_blk = _stack_offsets(mix)
    layout = tuple(_stack_offsets(st)[0] for st in addends)
    with_norm = g_next is not None
    row_spec = pl.BlockSpec((ROW_TILE, d), lambda m: (m, 0))
    in_specs = _stack_specs(mix, k, 0) + [
        pl.BlockSpec((None, k, d), lambda m: (layer, 0, 0), pipeline_mode=pl.Buffered(1)),
    ] + [sp for st in addends for sp in _stack_specs(st, d, 0)]
    operands = [*mix, w_out_bf16, *[a for st in addends for a in st]]
    out_specs = [row_spec]
    out_shape = [jax.ShapeDtypeStruct((n_blk * ROW_TILE, d), F32)]
    if with_norm:
        in_specs.append(pl.BlockSpec((None, 1, d), lambda m: (layer, 0, 0)))
        operands.append(g_next[:, None, :])
        out_specs.append(row_spec)
        out_shape.append(jax.ShapeDtypeStruct((n_blk * ROW_TILE, d), BF16))
    return pl.pallas_call(
        functools.partial(_outproj_body, mix_offs=mix_offs, layout=layout, with_norm=with_norm),
        grid=(n_blk,),
        in_specs=in_specs,
        out_specs=out_specs,
        out_shape=out_shape,
        compiler_params=_params(("arbitrary",)),
        name=f"outproj{layer}",
    )(*operands)


def _router_body(h_ref, g_ref, wr_ref, xn_ref, rank_ref, col_ref, cin_ref, cnt_ref, carry_ref):
    tb = h_ref.shape[0]

    @pl.when(pl.program_id(0) == 0)
    def _():
        carry_ref[...] = jnp.zeros_like(carry_ref)

    xn = _rms(h_ref[...], g_ref[...])
    xn_ref[...] = xn.astype(BF16)
    logits = lax.dot_general(wr_ref[...], xn, (((1,), (1,)), ((), ())),
                             precision=lax.Precision.HIGHEST, preferred_element_type=F32)
    eidx = lax.broadcasted_iota(I32, logits.shape, 0)
    m1 = jnp.max(logits, axis=0, keepdims=True)
    i1 = jnp.min(jnp.where(logits == m1, eidx, N_EXPERTS), axis=0, keepdims=True)
    sel1 = eidx == i1
    rest = jnp.where(sel1, -jnp.inf, logits)
    m2 = jnp.max(rest, axis=0, keepdims=True)
    i2 = jnp.min(jnp.where(rest == m2, eidx, N_EXPERTS), axis=0, keepdims=True)
    sel2 = eidx == i2
    e2 = jnp.exp(m2 - m1)
    g1 = 1.0 / (1.0 + e2)
    g2 = e2 / (1.0 + e2)
    gate = jnp.where(sel1, g1, jnp.where(sel2, g2, 0.0))
    routed = jnp.where(sel1, 1.0, jnp.where(sel2, 1.0, 0.0))
    before = lax.broadcasted_iota(I32, (tb, tb), 0) < lax.broadcasted_iota(I32, (tb, tb), 1)
    upper = jnp.where(before, 1.0, 0.0).astype(BF16)
    excl = jnp.dot(routed.astype(BF16), upper, preferred_element_type=F32)
    carry = carry_ref[...]
    rank = jnp.where(routed > 0.0, excl + carry[:, 0:1], -1.0)
    rank_ref[...] = rank.astype(I32)
    same = lax.broadcasted_iota(I32, (tb, tb), 0) == lax.broadcasted_iota(I32, (tb, tb), 1)
    col_ref[...] = lax.dot_general(jnp.where(same, 1.0, 0.0), jnp.concatenate([rank, gate], axis=0),
                                   (((1,), (1,)), ((), ())),
                                   precision=lax.Precision.HIGHEST, preferred_element_type=F32)
    cin_ref[...] = carry.astype(I32)
    new_carry = carry + jnp.sum(routed, axis=1, keepdims=True)
    carry_ref[...] = new_carry
    cnt_ref[...] = new_carry.astype(I32)


def _router(layer, h, g, w_router_t):
    t, d = h.shape
    nb = t // ROW_TILE
    e = w_router_t.shape[0]
    return pl.pallas_call(
        _router_body,
        grid=(nb,),
        in_specs=[pl.BlockSpec((ROW_TILE, d), lambda i: (i, 0)),
                  pl.BlockSpec((None, 1, d), lambda i: (layer, 0, 0)),
                  pl.BlockSpec((e, d), lambda i: (0, 0))],
        out_specs=[pl.BlockSpec((ROW_TILE, d), lambda i: (i, 0)),
                   pl.BlockSpec((e, ROW_TILE), lambda i: (0, i)),
                   pl.BlockSpec((ROW_TILE, 2 * e), lambda i: (i, 0)),
                   pl.BlockSpec((None, e, 128), lambda i: (i, 0, 0)),
                   pl.BlockSpec((e, 128), lambda i: (0, 0))],
        out_shape=[jax.ShapeDtypeStruct((t, d), BF16),
                   jax.ShapeDtypeStruct((e, t), I32),
                   jax.ShapeDtypeStruct((t, 2 * e), F32),
                   jax.ShapeDtypeStruct((nb, e, 128), I32),
                   jax.ShapeDtypeStruct((e, 128), I32)],
        scratch_shapes=[pltpu.VMEM((e, 128), F32)],
        compiler_params=_params(("arbitrary",)),
        name=f"router{layer}",
    )(h, g[:, None, :], w_router_t)


def _swiglu_blocks(x_ref, wg_ref, wu_ref, wd_ref, acc_ref, nsub):
    def rows_block(first_sub, n_sub):
        n_rows = n_sub * SUB_ROWS
        r0 = first_sub * SUB_ROWS
        if not isinstance(r0, int):
            r0 = pl.multiple_of(r0, SUB_ROWS)
        x = x_ref[pl.ds(r0, n_rows), :]
        g = jnp.dot(x, wg_ref[...].astype(BF16), preferred_element_type=F32)
        u = jnp.dot(x, wu_ref[...].astype(BF16), preferred_element_type=F32)
        a = (g * jax.nn.sigmoid(g) * u).astype(BF16)
        acc_ref[pl.ds(r0, n_rows), :] += jnp.dot(a, wd_ref[...].astype(BF16),
                                                 preferred_element_type=F32)

    if isinstance(nsub, int):
        for m in range(nsub // 4):
            rows_block(m * 4, 4)
        if nsub % 4 >= 2:
            rows_block((nsub // 4) * 4, 2)
        if nsub % 2 == 1:
            rows_block(nsub - 1, 1)
        return

    def quad(m, carry):
        rows_block(m * 4, 4)
        return carry

    lax.fori_loop(0, nsub // 4, quad, 0)

    @pl.when(nsub % 4 >= 2)
    def _():
        rows_block((nsub // 4) * 4, 2)

    @pl.when(nsub % 2 == 1)
    def _():
        rows_block(nsub - 1, 1)


def _dense_ffn_body(x_ref, wg_ref, wu_ref, wd_ref, o_ref):
    @pl.when(pl.program_id(1) == 0)
    def _():
        o_ref[...] = jnp.zeros_like(o_ref)

    _swiglu_blocks(x_ref, wg_ref, wu_ref, wd_ref, o_ref, x_ref.shape[0] // SUB_ROWS)


def _dense_ffn(layer, x, wg, wu, wd):
    rows, d = x.shape
    nf = wg.shape[2] // FF_TILE
    return pl.pallas_call(
        _dense_ffn_body,
        grid=(rows // DENSE_TILE, nf),
        in_specs=[pl.BlockSpec((DENSE_TILE, d), lambda s, f: (s, 0), pipeline_mode=pl.Buffered(1)),
                  pl.BlockSpec((None, d, FF_TILE), lambda s, f: (layer, 0, f)),
                  pl.BlockSpec((None, d, FF_TILE), lambda s, f: (layer, 0, f)),
                  pl.BlockSpec((None, FF_TILE, d), lambda s, f: (layer, f, 0))],
        out_specs=pl.BlockSpec((DENSE_TILE, d), lambda s, f: (s, 0), pipeline_mode=pl.Buffered(1)),
        out_shape=jax.ShapeDtypeStruct((rows, d), F32),
        compiler_params=_params(("arbitrary", "arbitrary")),
        name=f"ffn_dense{layer}",
    )(x, wg, wu, wd)


def _moe_ffn_body(te_ref, ns_ref, tk_ref, blo_ref, bhi_ref, cb_ref,
                  rank_ref, xn_hbm, wg_ref, wu_ref, wd_ref, y_hbm,
                  xs_ref, acc_ref, xbuf_ref, in_sem, out_sem, *, n_blocks):
    s = pl.program_id(0)
    f = pl.program_id(1)
    nsub = ns_ref[s]
    e = te_ref[s]
    tile_rows = xs_ref.shape[0]
    tb = xbuf_ref.shape[1]
    rel0 = tk_ref[s] * tile_rows

    def fetch(b, slot):
        src = xn_hbm.at[pl.ds(pl.multiple_of(b * tb, tb), tb), :]
        return pltpu.make_async_copy(src, xbuf_ref.at[slot], in_sem.at[slot])

    @pl.when((f == 0) & (nsub > 0))
    def _():
        blo = blo_ref[s]
        bhi = bhi_ref[s]
        for ahead in range(DMA_SLOTS - 1):
            @pl.when(blo + ahead <= bhi)
            def _():
                fetch(blo + ahead, ahead).start()

        acc_ref[...] = jnp.zeros_like(acc_ref)
        xs_ref[...] = jnp.zeros_like(xs_ref)

        def token_block(b, carry):
            slot = (b - blo) % DMA_SLOTS
            fetch(b, slot).wait()
            nxt = b + (DMA_SLOTS - 1)

            @pl.when(nxt <= bhi)
            def _():
                fetch(nxt, (nxt - blo) % DMA_SLOTS).start()

            c0 = cb_ref[e * (n_blocks + 1) + b]
            c1 = cb_ref[e * (n_blocks + 1) + b + 1]
            j_lo = jnp.clip((c0 - rel0) // SUB_ROWS, 0, nsub - 1)
            j_hi = jnp.clip((c1 - 1 - rel0) // SUB_ROWS, 0, nsub - 1)
            rank_row = rank_ref[pl.ds(e, 1), pl.ds(pl.multiple_of(b * tb, tb), tb)]
            xb = xbuf_ref.at[slot]

            def sub_block(j, c):
                r0 = pl.multiple_of(j * SUB_ROWS, SUB_ROWS)
                want = lax.broadcasted_iota(I32, (SUB_ROWS, tb), 0) + (rel0 + r0)
                onehot = jnp.where(want == rank_row, 1.0, 0.0).astype(BF16)
                xs_ref[pl.ds(r0, SUB_ROWS), :] += jnp.dot(
                    onehot, xb[...], preferred_element_type=F32).astype(BF16)
                return c

            lax.fori_loop(j_lo, j_hi + 1, sub_block, 0)
            return carry

        lax.fori_loop(blo, bhi + 1, token_block, 0)

    _swiglu_blocks(xs_ref, wg_ref, wu_ref, wd_ref, acc_ref, nsub)

    @pl.when(f == pl.num_programs(1) - 1)
    def _():
        @pl.when(nsub > 0)
        def _():
            xs_ref[...] = acc_ref[...].astype(BF16)

        @pl.when(nsub == 0)
        def _():
            xs_ref[...] = jnp.zeros_like(xs_ref)

        dst = y_hbm.at[pl.ds(pl.multiple_of(s * tile_rows, tile_rows), tile_rows), :]
        out = pltpu.make_async_copy(xs_ref, dst, out_sem.at[0])
        out.start()
        out.wait()

        @pl.when(s == pl.num_programs(0) - 1)
        def _():
            xs_ref[0:WINDOW_ROWS, :] = jnp.zeros((WINDOW_ROWS, xs_ref.shape[1]), BF16)
            end = y_hbm.shape[0] - WINDOW_ROWS
            tail = pltpu.make_async_copy(xs_ref.at[pl.ds(0, WINDOW_ROWS), :],
                                         y_hbm.at[pl.ds(end, WINDOW_ROWS), :], out_sem.at[0])
            tail.start()
            tail.wait()


def _moe_ffn(layer, xn, rank, plan, wg, wu, wd):
    tile_e, tile_nsub, tile_k, tile_blo, tile_bhi, cblk = plan
    t, d = xn.shape
    ff = wg.shape[3]
    nf = ff // FF_TILE
    n_tiles = tile_e.shape[0]
    nb = t // ROW_TILE

    def w_col(s, f, te, ns, *_):
        return (layer, te[s], 0, jnp.where(ns[s] > 0, f, nf - 1))

    def w_row(s, f, te, ns, *_):
        return (layer, te[s], jnp.where(ns[s] > 0, f, nf - 1), 0)

    return pl.pallas_call(
        functools.partial(_moe_ffn_body, n_blocks=nb),
        grid_spec=pltpu.PrefetchScalarGridSpec(
            num_scalar_prefetch=6,
            grid=(n_tiles, nf),
            in_specs=[pl.BlockSpec(rank.shape, lambda s, f, *_: (0, 0), pipeline_mode=pl.Buffered(1)),
                      pl.BlockSpec(memory_space=pl.ANY),
                      pl.BlockSpec((None, None, d, FF_TILE), w_col),
                      pl.BlockSpec((None, None, d, FF_TILE), w_col),
                      pl.BlockSpec((None, None, FF_TILE, d), w_row)],
            out_specs=pl.BlockSpec(memory_space=pl.ANY),
            scratch_shapes=[pltpu.VMEM((MOE_TILE, d), BF16),
                            pltpu.VMEM((MOE_TILE, d), F32),
                            pltpu.VMEM((DMA_SLOTS, ROW_TILE, d), BF16),
                            pltpu.SemaphoreType.DMA((DMA_SLOTS,)),
                            pltpu.SemaphoreType.DMA((1,))]),
        out_shape=jax.ShapeDtypeStruct((n_tiles * MOE_TILE + WINDOW_ROWS, d), BF16),
        compiler_params=_params(("arbitrary", "arbitrary")),
        name=f"ffn_moe{layer}",
    )(tile_e, tile_nsub, tile_k, tile_blo, tile_bhi, cblk.reshape(-1), rank, xn, wg, wu, wd)


def _combine_body(nwin_ref, ws_ref, we_ref, region_ref, col_ref, y_hbm, h_ref, gf_ref,
                  op_ref, os_ref, acc_ref, ybuf_ref, sem, *, prompt_blocks, max_windows):
    b = pl.program_id(0)
    tb = h_ref.shape[0]
    n = nwin_ref[b]
    base = b * max_windows

    def fetch(i, slot):
        start = pl.multiple_of(ws_ref[base + i], BF16_SUBLANES)
        return pltpu.make_async_copy(y_hbm.at[pl.ds(start, WINDOW_ROWS), :], ybuf_ref.at[slot],
                                     sem.at[slot])

    for ahead in range(DMA_SLOTS - 1):
        @pl.when(ahead < n)
        def _():
            fetch(ahead, ahead).start()

    acc_ref[...] = h_ref[...]

    def window(i, carry):
        slot = i % DMA_SLOTS
        fetch(i, slot).wait()
        nxt = i + (DMA_SLOTS - 1)

        @pl.when(nxt < n)
        def _():
            fetch(nxt, nxt % DMA_SLOTS).start()

        e = we_ref[base + i]
        cols = col_ref[...]
        lane = lax.broadcasted_iota(I32, cols.shape, 1)
        rank_col = jnp.sum(jnp.where(lane == e, cols, 0.0), axis=1, keepdims=True)
        gate_col = jnp.sum(jnp.where(lane == e + N_EXPERTS, cols, 0.0), axis=1, keepdims=True)
        first_rank = ws_ref[base + i] - region_ref[e]
        want = (lax.broadcasted_iota(I32, (tb, WINDOW_ROWS), 1) + first_rank).astype(F32)
        onehot = jnp.where(want == rank_col, 1.0, 0.0).astype(BF16)
        picked = jnp.dot(onehot, ybuf_ref[slot], preferred_element_type=F32)
        acc_ref[...] += gate_col * picked
        return carry

    lax.fori_loop(0, n, window, 0)

    res = _rms(acc_ref[...], gf_ref[...])

    @pl.when(b < prompt_blocks)
    def _():
        op_ref[...] = res

    @pl.when(b >= prompt_blocks)
    def _():
        os_ref[...] = res


def _combine(h, y, cols, plan, g_final, n_prompt):
    t, d = h.shape
    nwin, wstart, wexp, region = plan
    nb = t // ROW_TILE
    npb = n_prompt // ROW_TILE
    return pl.pallas_call(
        functools.partial(_combine_body, prompt_blocks=npb, max_windows=wstart.shape[0] // nb),
        grid_spec=pltpu.PrefetchScalarGridSpec(
            num_scalar_prefetch=4,
            grid=(nb,),
            in_specs=[pl.BlockSpec((ROW_TILE, cols.shape[1]), lambda b, *_: (b, 0)),
                      pl.BlockSpec(memory_space=pl.ANY),
                      pl.BlockSpec((ROW_TILE, d), lambda b, *_: (b, 0)),
                      pl.BlockSpec((1, d), lambda b, *_: (0, 0))],
            out_specs=[pl.BlockSpec((ROW_TILE, d), lambda b, *_: (jnp.minimum(b, npb - 1), 0)),
                       pl.BlockSpec((ROW_TILE, d), lambda b, *_: (jnp.maximum(b - npb, 0), 0))],
            scratch_shapes=[pltpu.VMEM((ROW_TILE, d), F32),
                            pltpu.VMEM((DMA_SLOTS, WINDOW_ROWS, d), BF16),
                            pltpu.SemaphoreType.DMA((DMA_SLOTS,))]),
        out_shape=[jax.ShapeDtypeStruct((n_prompt, d), F32),
                   jax.ShapeDtypeStruct((t - n_prompt, d), F32)],
        compiler_params=_params(("arbitrary",)),
        name="moe_combine",
    )(nwin, wstart, wexp, region, cols, y, h, g_final[None, :])


def _owner(cum, q):
    return jnp.sum(cum[None, :] <= q[:, None], axis=1).astype(I32)


def _moe_plan(cin, cnt, n_tok):
    nb = n_tok // ROW_TILE
    n_slots = n_tok * TOP_K
    s_max = n_slots // MOE_TILE + N_EXPERTS
    counts = cnt[:, 0]
    cblk = jnp.concatenate([cin[:, :, 0].T, counts[:, None]], axis=1)
    ntile = (counts + MOE_TILE - 1) // MOE_TILE
    tcum = jnp.cumsum(ntile)
    tstart = tcum - ntile
    total = tcum[-1]
    s_idx = jnp.arange(s_max, dtype=I32)
    tile_ok = s_idx < total
    tile_e = _owner(tcum, jnp.minimum(s_idx, total - 1))
    tile_k = s_idx - tstart[tile_e]
    tile_rows = jnp.where(tile_ok, jnp.clip(counts[tile_e] - tile_k * MOE_TILE, 0, MOE_TILE), 0)
    tile_nsub = ((tile_rows + SUB_ROWS - 1) // SUB_ROWS).astype(I32)
    region = (tstart * MOE_TILE).astype(I32)

    rel = tile_k * MOE_TILE
    cb = cblk[tile_e]
    b_lo = jnp.clip(jnp.sum(cb[:, 1:] <= rel[:, None], axis=1), 0, nb - 1)
    b_hi = jnp.sum(cb[:, :-1] < (rel + tile_rows)[:, None], axis=1) - 1
    b_hi = jnp.clip(jnp.maximum(b_hi, b_lo), 0, nb - 1)
    fplan = (tile_e, tile_nsub, tile_k.astype(I32), b_lo.astype(I32), b_hi.astype(I32),
             cblk.astype(I32))

    span = (cblk[:, 1:] - cblk[:, :-1]).T
    start = (region[:, None] + cblk[:, :-1]).T
    first = (start // BF16_SUBLANES) * BF16_SUBLANES
    nwin = jnp.where(span > 0, (start + span - first + WINDOW_ROWS - 1) // WINDOW_ROWS, 0)
    wcum = jnp.cumsum(nwin, axis=1)
    wfirst = wcum - nwin
    max_windows = N_EXPERTS * (ROW_TILE // WINDOW_ROWS + 1)
    q = jnp.arange(max_windows, dtype=I32)[None, :]
    q = jnp.minimum(q, wcum[:, -1:] - 1)
    wexp = jnp.sum(wcum[:, None, :] <= q[:, :, None], axis=2).astype(I32)
    k = q - jnp.take_along_axis(wfirst, wexp, axis=1)
    wstart = jnp.take_along_axis(first, wexp, axis=1) + k * WINDOW_ROWS
    cplan = (wcum[:, -1].astype(I32), wstart.reshape(-1).astype(I32), wexp.reshape(-1), region)
    return fplan, cplan


def kernel(x_prompt, x_sample, state_pool, g_mix, w_in, g_v, w_pool, pool_scale, w_s, b_s, w_out,
           g_ffn, dense_w_gate, dense_w_up, dense_w_down, w_router, moe_w_gate, moe_w_up,
           moe_w_down, g_final):
    batch, seq, d = x_prompt.shape
    dec_batch, dec_seq, _ = x_sample.shape
    depth = w_in.shape[0]
    n_p = batch * seq
    n_s = dec_batch * dec_seq
    n_tok = n_p + n_s
    assert dec_seq == 8 and depth == 2 and n_p % ROW_TILE == 0 and n_s % ROW_TILE == 0

    addends = ((x_prompt.reshape(n_p, d), x_sample.reshape(n_s, d)),)
    dn = w_in.shape[2] // 3
    hd = dn // N_HEADS
    per_chunk = CHUNK // dec_seq
    pools_p, pools_s, vs_s = [], [], []
    out = None
    w_in_bf16 = w_in.astype(BF16)
    w_out_bf16 = w_out.astype(BF16)
    for i in range(depth):
        p, u, v = _inproj(i, addends, g_mix, w_in_bf16, g_v)
        bias_p = jnp.repeat(b_s[i].T, hd, axis=1)
        ws_s = w_s[i][:, :dec_seq, :dec_seq]
        eye = jnp.eye(per_chunk, dtype=F32)
        wm_s = (eye[None, :, None, :, None] * ws_s[:, None, :, None, :]).reshape(N_HEADS, CHUNK, CHUNK)
        bias_s = jnp.tile(jnp.repeat(b_s[i][:, :dec_seq].T, hd, axis=1), (per_chunk, 1))
        mix_p = _mixer(i, p, u, v, 0, n_p, seq, 0, w_s[i], bias_p, w_pool, pool_scale)
        mix_s = _mixer(i, p, u, v, n_p, n_s, dec_seq, PAST_LEN, wm_s, bias_s, w_pool, pool_scale,
                       state=state_pool)
        dense = i % 2 == 0
        h, *xn = _outproj(i, (mix_p, mix_s), w_out_bf16, addends, g_next=g_ffn if dense else None)

        pools_p.append(jnp.stack([p[(b + 1) * seq - POOL_BUF:(b + 1) * seq] for b in range(batch)]))
        pools_s.append(jnp.concatenate(
            [state_pool[i, :, dec_seq:], p[n_p:].reshape(dec_batch, dec_seq, dn)], axis=1))
        vs_s.append(v[n_p:].reshape(dec_batch, dec_seq, dn))

        j = i // 2
        if dense:
            y = _dense_ffn(j, xn[0], dense_w_gate, dense_w_up, dense_w_down)
            addends = ((h,), (y,))
        else:
            xn, rank, cols, cin, cnt = _router(i, h, g_ffn, w_router[j].T)
            fplan, cplan = _moe_plan(cin, cnt, n_tok)
            y = _moe_ffn(j, xn, rank, fplan, moe_w_gate, moe_w_up, moe_w_down)
            out = _combine(h, y, cols, cplan, g_final, n_p)
            addends = (tuple(out),)

    y_prompt = out[0].reshape(batch, seq, d)
    y_sample = out[1].reshape(dec_batch, dec_seq, d)
    return (y_prompt, y_sample, jnp.stack(pools_p), jnp.stack(pools_s), jnp.stack(vs_s))
```

```python
import functools

import jax
import jax.numpy as jnp
from jax import lax
from jax.experimental import pallas as pl
from jax.experimental.pallas import tpu as pltpu

F32 = jnp.float32
BF16 = jnp.bfloat16
I32 = jnp.int32

EPS = 1e-6
POOL_WINDOWS = (2, 4, 8, 16)
POOL_BUF = max(POOL_WINDOWS) - 1
HALO = 16
CHUNK = 128
N_HEADS = 8
N_EXPERTS = 8
TOP_K = 2
PAST_LEN = 16384

V7X_VMEM_LIMIT = 56 * 1024 * 1024
V7X_MXU_COLS = 256

ROW_TILE = 512
FF_TILE = 256
SUB_ROWS = 256
MOE_TILE = 2560
WINDOW_ROWS = 256
BF16_SUBLANES = 16
DMA_SLOTS = 4
DENSE_TILE = 2304


def _params(sem):
    return pltpu.CompilerParams(dimension_semantics=sem, vmem_limit_bytes=V7X_VMEM_LIMIT)


def _rms(x, g):
    return x * lax.rsqrt(jnp.mean(x * x, axis=-1, keepdims=True) + EPS) * g


def _gelu(x):
    return 0.5 * x * (1.0 + lax.erf(x * (2.0 ** -0.5)))


def _stack_offsets(stack):
    offs, o = [], 0
    for a in stack:
        offs.append(o)
        o += a.shape[0] // ROW_TILE
    return tuple(offs), o


def _stack_specs(stack, cols, m_axis, col_fn=None):
    offs, _ = _stack_offsets(stack)
    specs = []
    for a, off in zip(stack, offs):
        nblk = a.shape[0] // ROW_TILE

        def imap(*idx, off=off, nblk=nblk):
            col = 0 if col_fn is None else col_fn(*idx)
            return (jnp.clip(idx[m_axis] - off, 0, nblk - 1), col)

        specs.append(pl.BlockSpec((ROW_TILE, cols), imap))
    return specs


def _stack_read(refs, offs, m):
    v = refs[0][...]
    for ref, off in zip(refs[1:], offs[1:]):
        v = jnp.where(m >= off, ref[...], v)
    return v


def _sum_read(refs, layout, m):
    total, k = None, 0
    for offs in layout:
        part = _stack_read(refs[k:k + len(offs)], offs, m)
        k += len(offs)
        total = part if total is None else total + part
    return total


def _inproj_body(*refs, layout, sample_block0):
    n_x = sum(len(offs) for offs in layout)
    g_ref, w_ref, gv_ref, p_ref, u_ref, v_ref, vs_ref, vtmp_ref = refs[n_x:]
    m = pl.program_id(0)
    x = _sum_read(refs[:n_x], layout, m)
    rowscale = lax.rsqrt(jnp.mean(x * x, axis=-1, keepdims=True) + EPS)
    xg = (x * g_ref[...]).astype(BF16)
    dn = p_ref.shape[1]
    ss = None
    for k, o_ref in enumerate((p_ref, u_ref, vtmp_ref)):
        for c0 in range(0, dn, V7X_MXU_COLS):
            w = w_ref[:, k * dn + c0:k * dn + c0 + V7X_MXU_COLS]
            acc = jnp.dot(xg, w, preferred_element_type=F32) * rowscale
            if o_ref is not p_ref:
                acc = _gelu(acc)
            if o_ref is vtmp_ref:
                part = jnp.sum(acc * acc, axis=1, keepdims=True)
                ss = part if ss is None else ss + part
            o_ref[:, c0:c0 + V7X_MXU_COLS] = acc.astype(o_ref.dtype)
    vn = vtmp_ref[...] * lax.rsqrt(ss / dn + EPS) * gv_ref[...]
    v_ref[...] = vn.astype(BF16)

    @pl.when(m >= sample_block0)
    def _():
        vs_ref[...] = vn


def _inproj(layer, addends, g_mix, w_in_bf16, g_v, n_prompt):
    d = addends[0][0].shape[1]
    dn = w_in_bf16.shape[2] // 3
    layout = tuple(_stack_offsets(st)[0] for st in addends)
    n_blk = _stack_offsets(addends[0])[1]
    x_specs = [sp for st in addends for sp in _stack_specs(st, d, 0)]
    out_spec = pl.BlockSpec((ROW_TILE, dn), lambda m: (m, 0))
    n_tok = n_blk * ROW_TILE
    npb = n_prompt // ROW_TILE
    return pl.pallas_call(
        functools.partial(_inproj_body, layout=layout, sample_block0=npb),
        grid=(n_blk,),
        in_specs=x_specs + [
            pl.BlockSpec((None, 1, d), lambda m: (layer, 0, 0)),
            pl.BlockSpec((None, d, 3 * dn), lambda m: (layer, 0, 0), pipeline_mode=pl.Buffered(1)),
            pl.BlockSpec((None, 1, dn), lambda m: (layer, 0, 0)),
        ],
        out_specs=[out_spec] * 3 + [
            pl.BlockSpec((ROW_TILE, dn), lambda m: (jnp.maximum(m - npb, 0), 0))],
        out_shape=[jax.ShapeDtypeStruct((n_tok, dn), F32),
                   jax.ShapeDtypeStruct((n_tok, dn), BF16),
                   jax.ShapeDtypeStruct((n_tok, dn), BF16),
                   jax.ShapeDtypeStruct((n_tok - n_prompt, dn), F32)],
        scratch_shapes=[pltpu.VMEM((ROW_TILE, dn), F32)],
        compiler_params=_params(("arbitrary",)),
        name=f"inproj{layer}",
    )(*[a for st in addends for a in st], g_mix[:, None, :], w_in_bf16, g_v[:, None, :])


def _mixer_body(*refs, rows, blocks_per_seq, start_pos, sample):
    p_ref, hist_ref, u_ref, v_ref, wm_ref, bias_ref, wp_ref, ps_ref, o_ref, full_ref = refs
    d_pool = p_ref.shape[1]
    gd = d_pool // len(POOL_WINDOWS)
    hd = u_ref.shape[1] // N_HEADS
    blk = pl.program_id(0) % blocks_per_seq
    p = p_ref[...]

    if sample:
        seqs = rows // 8
        full_ref[:, 1:HALO, :] = hist_ref[...]
        full_ref[:, HALO:HALO + 8, :] = p.reshape(seqs, 8, d_pool)
    else:
        full_ref[0:HALO, :] = jnp.where(blk == 0, 0.0, hist_ref[...])
        full_ref[HALO:HALO + rows, :] = p
    row_id = lax.broadcasted_iota(I32, (rows, 1), 0)
    pos = start_pos + (row_id % 8 if sample else blk * rows + row_id)
    for gi, w in enumerate(POOL_WINDOWS):
        c0 = gi * gd
        s = None
        for j in range(w):
            if sample:
                term = full_ref[:, HALO - j:HALO - j + 8, c0:c0 + gd].reshape(rows, gd)
            else:
                term = full_ref[HALO - j:HALO - j + rows, c0:c0 + gd]
            s = term if s is None else s + term
        cnt = jnp.minimum(pos + 1, w).astype(F32)
        r = s / cnt - p[:, c0:c0 + gd]
        y = jnp.dot(r.astype(BF16), wp_ref[gi].astype(BF16), preferred_element_type=F32)
        o_ref[:, c0:c0 + gd] = (y * ps_ref[:, c0:c0 + gd]).astype(BF16)

    row = lax.broadcasted_iota(I32, (CHUNK, CHUNK), 0)
    col = lax.broadcasted_iota(I32, (CHUNK, CHUNK), 1)
    for h in range(N_HEADS):
        c0 = h * hd
        wm = jnp.where(row >= col, wm_ref[h], 0.0).astype(BF16)
        for c in range(rows // CHUNK):
            r0 = c * CHUNK
            z = jnp.dot(wm, v_ref[r0:r0 + CHUNK, c0:c0 + hd].astype(BF16),
                        preferred_element_type=F32) + bias_ref[:, c0:c0 + hd]
            o_ref[r0:r0 + CHUNK, d_pool + c0:d_pool + c0 + hd] = (
                u_ref[r0:r0 + CHUNK, c0:c0 + hd] * z).astype(BF16)


def _mixer(layer, p, u, v, row0, n_rows, seq_len, start_pos, wm, bias, w_pool, pool_scale,
           state=None):
    dn = p.shape[1]
    sample = state is not None
    rows = ROW_TILE
    blocks_per_seq = 1 if sample else seq_len // rows
    b0 = row0 // rows
    blk_spec = pl.BlockSpec((rows, dn), lambda i: (b0 + i, 0))
    if sample:
        hist_spec = pl.BlockSpec((None, rows // 8, POOL_BUF, dn), lambda i: (layer, i, 0, 0))
        hist = state
        scratch = pltpu.VMEM((rows // 8, HALO + 8, dn), F32)
    else:
        per = rows // HALO
        hist_spec = pl.BlockSpec((HALO, dn), lambda i: (jnp.maximum((b0 + i) * per - 1, 0), 0))
        hist = p
        scratch = pltpu.VMEM((HALO + rows, dn), F32)
    return pl.pallas_call(
        functools.partial(_mixer_body, rows=rows, blocks_per_seq=blocks_per_seq,
                          start_pos=start_pos, sample=sample),
        grid=(n_rows // rows,),
        in_specs=[
            blk_spec, hist_spec, blk_spec, blk_spec,
            pl.BlockSpec((N_HEADS, CHUNK, CHUNK), lambda i: (0, 0, 0)),
            pl.BlockSpec((CHUNK, dn), lambda i: (0, 0)),
            pl.BlockSpec((None,) + w_pool.shape[1:], lambda i: (layer, 0, 0, 0)),
            pl.BlockSpec((None, 1, dn), lambda i: (layer, 0, 0)),
        ],
        out_specs=pl.BlockSpec((rows, 2 * dn), lambda i: (i, 0)),
        out_shape=jax.ShapeDtypeStruct((n_rows, 2 * dn), BF16),
        scratch_shapes=[scratch],
        compiler_params=_params(("arbitrary",)),
        name=f"mixer{layer}_{'sample' if sample else 'prompt'}",
    )(p, hist, u, v, wm, bias, w_pool, pool_scale[:, None, :])


def _outproj_body(*refs, mix_offs, layout, with_norm):
    n_mix = len(mix_offs)
    n_h = sum(len(offs) for offs in layout)
    w_ref = refs[n_mix]
    h_refs = refs[n_mix + 1:n_mix + 1 + n_h]
    if with_norm:
        g_ref, o_ref, xn_ref = refs[n_mix + 1 + n_h:]
    else:
        o_ref, = refs[n_mix + 1 + n_h:]
    m = pl.program_id(0)
    a = _stack_read(refs[:n_mix], mix_offs, m)
    h = _sum_read(h_refs, layout, m)
    d = o_ref.shape[1]
    ss = None
    for c0 in range(0, d, V7X_MXU_COLS):
        cols = slice(c0, c0 + V7X_MXU_COLS)
        hn = h[:, cols] + jnp.dot(a, w_ref[:, cols], preferred_element_type=F32)
        o_ref[:, cols] = hn
        if with_norm:
            part = jnp.sum(hn * hn, axis=1, keepdims=True)
            ss = part if ss is None else ss + part
    if with_norm:
        xn_ref[...] = (o_ref[...] * lax.rsqrt(ss / d + EPS) * g_ref[...]).astype(BF16)


def _outproj(layer, mix, w_out_bf16, addends, g_next=None):
    k = mix[0].shape[1]
    d = w_out_bf16.shape[2]
    mix_offs, n_blk = _stack_offsets(mix)
    layout = tuple(_stack_offsets(st)[0] for st in addends)
    with_norm = g_next is not None
    row_spec = pl.BlockSpec((ROW_TILE, d), lambda m: (m, 0))
    in_specs = _stack_specs(mix, k, 0) + [
        pl.BlockSpec((None, k, d), lambda m: (layer, 0, 0), pipeline_mode=pl.Buffered(1)),
    ] + [sp for st in addends for sp in _stack_specs(st, d, 0)]
    operands = [*mix, w_out_bf16, *[a for st in addends for a in st]]
    out_specs = [row_spec]
    out_shape = [jax.ShapeDtypeStruct((n_blk * ROW_TILE, d), F32)]
    if with_norm:
        in_specs.append(pl.BlockSpec((None, 1, d), lambda m: (layer, 0, 0)))
        operands.append(g_next[:, None, :])
        out_specs.append(row_spec)
        out_shape.append(jax.ShapeDtypeStruct((n_blk * ROW_TILE, d), BF16))
    return pl.pallas_call(
        functools.partial(_outproj_body, mix_offs=mix_offs, layout=layout, with_norm=with_norm),
        grid=(n_blk,),
        in_specs=in_specs,
        out_specs=out_specs,
        out_shape=out_shape,
        compiler_params=_params(("arbitrary",)),
        name=f"outproj{layer}",
    )(*operands)


def _split_bf16(x, pieces):
    out = []
    for _ in range(pieces):
        term = x.astype(BF16)
        out.append(term)
        x = x - term.astype(F32)
    return out


def _dot_nt(a, b):
    return lax.dot_general(a, b, (((1,), (1,)), ((), ())), preferred_element_type=F32)


def _router_body(h_ref, g_ref, wr_ref, xn_ref, rank_ref, col_ref, cin_ref, cnt_ref,
                 carry_ref, upper_ref, ident_ref):
    tb = h_ref.shape[0]

    @pl.when(pl.program_id(0) == 0)
    def _():
        carry_ref[...] = jnp.zeros_like(carry_ref)
        row = lax.broadcasted_iota(I32, (tb, tb), 0)
        col = lax.broadcasted_iota(I32, (tb, tb), 1)
        upper_ref[...] = jnp.where(row < col, 1.0, 0.0).astype(BF16)
        ident_ref[...] = jnp.where(row == col, 1.0, 0.0).astype(BF16)

    xn = _rms(h_ref[...], g_ref[...])
    x_hi, x_lo = _split_bf16(xn, 2)
    xn_ref[...] = x_hi
    w_hi, w_lo = _split_bf16(wr_ref[...], 2)
    logits = _dot_nt(w_hi, x_hi) + (_dot_nt(w_hi, x_lo) + _dot_nt(w_lo, x_hi))
    eidx = lax.broadcasted_iota(I32, logits.shape, 0)
    m1 = jnp.max(logits, axis=0, keepdims=True)
    i1 = jnp.min(jnp.where(logits == m1, eidx, N_EXPERTS), axis=0, keepdims=True)
    sel1 = eidx == i1
    rest = jnp.where(sel1, -jnp.inf, logits)
    m2 = jnp.max(rest, axis=0, keepdims=True)
    i2 = jnp.min(jnp.where(rest == m2, eidx, N_EXPERTS), axis=0, keepdims=True)
    sel2 = eidx == i2
    e2 = jnp.exp(m2 - m1)
    g1 = 1.0 / (1.0 + e2)
    g2 = e2 / (1.0 + e2)
    gate = jnp.where(sel1, g1, jnp.where(sel2, g2, 0.0))
    routed = jnp.where(sel1, 1.0, jnp.where(sel2, 1.0, 0.0))
    excl = jnp.dot(routed.astype(BF16), upper_ref[...], preferred_element_type=F32)
    carry = carry_ref[...]
    rank = jnp.where(routed > 0.0, excl + carry[:, 0:1], -1.0)
    rank_ref[...] = rank.astype(I32)
    ident = ident_ref[...]
    d_hi, d_mid, d_lo = _split_bf16(jnp.concatenate([rank, gate], axis=0), 3)
    col_ref[...] = (_dot_nt(ident, d_hi) + _dot_nt(ident, d_mid)) + _dot_nt(ident, d_lo)
    cin_ref[...] = carry.astype(I32)
    new_carry = carry + jnp.sum(routed, axis=1, keepdims=True)
    carry_ref[...] = new_carry
    cnt_ref[...] = new_carry.astype(I32)


def _router(layer, h, g, w_router_t):
    t, d = h.shape
    nb = t // ROW_TILE
    e = w_router_t.shape[0]
    return pl.pallas_call(
        _router_body,
        grid=(nb,),
        in_specs=[pl.BlockSpec((ROW_TILE, d), lambda i: (i, 0)),
                  pl.BlockSpec((None, 1, d), lambda i: (layer, 0, 0)),
                  pl.BlockSpec((e, d), lambda i: (0, 0))],
        out_specs=[pl.BlockSpec((ROW_TILE, d), lambda i: (i, 0)),
                   pl.BlockSpec((e, ROW_TILE), lambda i: (0, i)),
                   pl.BlockSpec((ROW_TILE, 2 * e), lambda i: (i, 0)),
                   pl.BlockSpec((None, e, 128), lambda i: (i, 0, 0)),
                   pl.BlockSpec((e, 128), lambda i: (0, 0))],
        out_shape=[jax.ShapeDtypeStruct((t, d), BF16),
                   jax.ShapeDtypeStruct((e, t), I32),
                   jax.ShapeDtypeStruct((t, 2 * e), F32),
                   jax.ShapeDtypeStruct((nb, e, 128), I32),
                   jax.ShapeDtypeStruct((e, 128), I32)],
        scratch_shapes=[pltpu.VMEM((e, 128), F32), pltpu.VMEM((ROW_TILE, ROW_TILE), BF16),
                        pltpu.VMEM((ROW_TILE, ROW_TILE), BF16)],
        compiler_params=_params(("arbitrary",)),
        name=f"router{layer}",
    )(h, g[:, None, :], w_router_t)


def _swiglu_blocks(x_ref, wg_ref, wu_ref, wd_ref, acc_ref, nsub):
    def rows_block(first_sub, n_sub):
        n_rows = n_sub * SUB_ROWS
        r0 = first_sub * SUB_ROWS
        if not isinstance(r0, int):
            r0 = pl.multiple_of(r0, SUB_ROWS)
        x = x_ref[pl.ds(r0, n_rows), :]
        g = jnp.dot(x, wg_ref[...].astype(BF16), preferred_element_type=F32)
        u = jnp.dot(x, wu_ref[...].astype(BF16), preferred_element_type=F32)
        a = (g * jax.nn.sigmoid(g) * u).astype(BF16)
        acc_ref[pl.ds(r0, n_rows), :] += jnp.dot(a, wd_ref[...].astype(BF16),
                                                 preferred_element_type=F32)

    if isinstance(nsub, int):
        for m in range(nsub // 4):
            rows_block(m * 4, 4)
        if nsub % 4 >= 2:
            rows_block((nsub // 4) * 4, 2)
        if nsub % 2 == 1:
            rows_block(nsub - 1, 1)
        return

    def quad(m, carry):
        rows_block(m * 4, 4)
        return carry

    lax.fori_loop(0, nsub // 4, quad, 0)

    @pl.when(nsub % 4 >= 2)
    def _():
        rows_block((nsub // 4) * 4, 2)

    @pl.when(nsub % 2 == 1)
    def _():
        rows_block(nsub - 1, 1)


def _dense_ffn_body(x_ref, wg_ref, wu_ref, wd_ref, o_ref):
    @pl.when(pl.program_id(1) == 0)
    def _():
        o_ref[...] = jnp.zeros_like(o_ref)

    _swiglu_blocks(x_ref, wg_ref, wu_ref, wd_ref, o_ref, x_ref.shape[0] // SUB_ROWS)


def _dense_ffn(layer, x, wg, wu, wd):
    rows, d = x.shape
    nf = wg.shape[2] // FF_TILE
    return pl.pallas_call(
        _dense_ffn_body,
        grid=(rows // DENSE_TILE, nf),
        in_specs=[pl.BlockSpec((DENSE_TILE, d), lambda s, f: (s, 0), pipeline_mode=pl.Buffered(1)),
                  pl.BlockSpec((None, d, FF_TILE), lambda s, f: (layer, 0, f)),
                  pl.BlockSpec((None, d, FF_TILE), lambda s, f: (layer, 0, f)),
                  pl.BlockSpec((None, FF_TILE, d), lambda s, f: (layer, f, 0))],
        out_specs=pl.BlockSpec((DENSE_TILE, d), lambda s, f: (s, 0), pipeline_mode=pl.Buffered(1)),
        out_shape=jax.ShapeDtypeStruct((rows, d), F32),
        compiler_params=_params(("arbitrary", "arbitrary")),
        name=f"ffn_dense{layer}",
    )(x, wg, wu, wd)


def _moe_ffn_body(te_ref, ns_ref, tk_ref, blo_ref, bhi_ref, cb_ref,
                  rank_ref, xn_hbm, wg_ref, wu_ref, wd_ref, y_hbm,
                  xs_ref, acc_ref, xbuf_ref, in_sem, out_sem, *, n_blocks, n_tiles):
    s = pl.program_id(0)
    f = pl.program_id(1)
    nsub = ns_ref[s]
    e = te_ref[s]
    tile_rows = xs_ref.shape[0]
    tb = xbuf_ref.shape[1]
    rel0 = tk_ref[s] * tile_rows

    def fetch(b, slot):
        src = xn_hbm.at[pl.ds(pl.multiple_of(b * tb, tb), tb), :]
        return pltpu.make_async_copy(src, xbuf_ref.at[slot], in_sem.at[slot])

    @pl.when((f == 0) & (nsub > 0))
    def _():
        blo = blo_ref[s]
        bhi = bhi_ref[s]
        for ahead in range(DMA_SLOTS - 1):
            @pl.when(blo + ahead <= bhi)
            def _():
                fetch(blo + ahead, ahead).start()

        acc_ref[...] = jnp.zeros_like(acc_ref)
        xs_ref[...] = jnp.zeros_like(xs_ref)

        def token_block(b, carry):
            slot = (b - blo) % DMA_SLOTS
            fetch(b, slot).wait()
            nxt = b + (DMA_SLOTS - 1)

            @pl.when(nxt <= bhi)
            def _():
                fetch(nxt, (nxt - blo) % DMA_SLOTS).start()

            c0 = cb_ref[e * (n_blocks + 1) + b]
            c1 = cb_ref[e * (n_blocks + 1) + b + 1]
            j_lo = jnp.clip((c0 - rel0) // SUB_ROWS, 0, nsub - 1)
            j_hi = jnp.clip((c1 - 1 - rel0) // SUB_ROWS, 0, nsub - 1)
            rank_row = rank_ref[pl.ds(e, 1), pl.ds(pl.multiple_of(b * tb, tb), tb)]
            xb = xbuf_ref.at[slot]

            def sub_block(j, c):
                r0 = pl.multiple_of(j * SUB_ROWS, SUB_ROWS)
                want = lax.broadcasted_iota(I32, (SUB_ROWS, tb), 0) + (rel0 + r0)
                onehot = jnp.where(want == rank_row, 1.0, 0.0).astype(BF16)
                xs_ref[pl.ds(r0, SUB_ROWS), :] += jnp.dot(
                    onehot, xb[...], preferred_element_type=F32).astype(BF16)
                return c

            lax.fori_loop(j_lo, j_hi + 1, sub_block, 0)
            return carry

        lax.fori_loop(blo, bhi + 1, token_block, 0)

    _swiglu_blocks(xs_ref, wg_ref, wu_ref, wd_ref, acc_ref, nsub)

    @pl.when(f == pl.num_programs(1) - 1)
    def _():
        @pl.when(nsub > 0)
        def _():
            xs_ref[...] = acc_ref[...].astype(BF16)

        @pl.when(nsub == 0)
        def _():
            xs_ref[...] = jnp.zeros_like(xs_ref)

        dst = y_hbm.at[pl.ds(pl.multiple_of(s * tile_rows, tile_rows), tile_rows), :]
        out = pltpu.make_async_copy(xs_ref, dst, out_sem.at[0])
        out.start()
        out.wait()

        @pl.when(s == n_tiles - 1)
        def _():
            xs_ref[...] = jnp.zeros_like(xs_ref)
            max_tiles = y_hbm.shape[0] // tile_rows
            for k in range(n_tiles, max_tiles):
                rest = pltpu.make_async_copy(
                    xs_ref, y_hbm.at[pl.ds(k * tile_rows, tile_rows), :], out_sem.at[0])
                rest.start()
                rest.wait()
            tail = pltpu.make_async_copy(
                xs_ref.at[pl.ds(0, WINDOW_ROWS), :],
                y_hbm.at[pl.ds(max_tiles * tile_rows, WINDOW_ROWS), :], out_sem.at[0])
            tail.start()
            tail.wait()


def _moe_ffn(layer, xn, rank, plan, wg, wu, wd, n_tiles):
    tile_e, tile_nsub, tile_k, tile_blo, tile_bhi, cblk = plan
    t, d = xn.shape
    ff = wg.shape[3]
    nf = ff // FF_TILE
    max_tiles = tile_e.shape[0]
    nb = t // ROW_TILE

    def w_col(s, f, te, ns, *_):
        return (layer, te[s], 0, jnp.where(ns[s] > 0, f, nf - 1))

    def w_row(s, f, te, ns, *_):
        return (layer, te[s], jnp.where(ns[s] > 0, f, nf - 1), 0)

    return pl.pallas_call(
        functools.partial(_moe_ffn_body, n_blocks=nb, n_tiles=n_tiles),
        grid_spec=pltpu.PrefetchScalarGridSpec(
            num_scalar_prefetch=6,
            grid=(n_tiles, nf),
            in_specs=[pl.BlockSpec(rank.shape, lambda s, f, *_: (0, 0), pipeline_mode=pl.Buffered(1)),
                      pl.BlockSpec(memory_space=pl.ANY),
                      pl.BlockSpec((None, None, d, FF_TILE), w_col),
                      pl.BlockSpec((None, None, d, FF_TILE), w_col),
                      pl.BlockSpec((None, None, FF_TILE, d), w_row)],
            out_specs=pl.BlockSpec(memory_space=pl.ANY),
            scratch_shapes=[pltpu.VMEM((MOE_TILE, d), BF16),
                            pltpu.VMEM((MOE_TILE, d), F32),
                            pltpu.VMEM((DMA_SLOTS, ROW_TILE, d), BF16),
                            pltpu.SemaphoreType.DMA((DMA_SLOTS,)),
                            pltpu.SemaphoreType.DMA((1,))]),
        out_shape=jax.ShapeDtypeStruct((max_tiles * MOE_TILE + WINDOW_ROWS, d), BF16),
        compiler_params=_params(("arbitrary", "arbitrary")),
        name=f"ffn_moe{layer}",
    )(tile_e, tile_nsub, tile_k, tile_blo, tile_bhi, cblk.reshape(-1), rank, xn, wg, wu, wd)


def _combine_body(nwin_ref, ws_ref, we_ref, region_ref, col_ref, y_hbm, h_ref, gf_ref,
                  op_ref, os_ref, acc_ref, ybuf_ref, sem, *, prompt_blocks, max_windows):
    b = pl.program_id(0)
    tb = h_ref.shape[0]
    n = nwin_ref[b]
    base = b * max_windows

    def fetch(i, slot):
        start = pl.multiple_of(ws_ref[base + i], BF16_SUBLANES)
        return pltpu.make_async_copy(y_hbm.at[pl.ds(start, WINDOW_ROWS), :], ybuf_ref.at[slot],
                                     sem.at[slot])

    for ahead in range(DMA_SLOTS - 1):
        @pl.when(ahead < n)
        def _():
            fetch(ahead, ahead).start()

    acc_ref[...] = h_ref[...]

    def window(i, carry):
        slot = i % DMA_SLOTS
        fetch(i, slot).wait()
        nxt = i + (DMA_SLOTS - 1)

        @pl.when(nxt < n)
        def _():
            fetch(nxt, nxt % DMA_SLOTS).start()

        e = we_ref[base + i]
        cols = col_ref[...]
        lane = lax.broadcasted_iota(I32, cols.shape, 1)
        rank_col = jnp.sum(jnp.where(lane == e, cols, 0.0), axis=1, keepdims=True)
        gate_col = jnp.sum(jnp.where(lane == e + N_EXPERTS, cols, 0.0), axis=1, keepdims=True)
        first_rank = ws_ref[base + i] - region_ref[e]
        want = (lax.broadcasted_iota(I32, (tb, WINDOW_ROWS), 1) + first_rank).astype(F32)
        onehot = jnp.where(want == rank_col, 1.0, 0.0).astype(BF16)
        picked = jnp.dot(onehot, ybuf_ref[slot], preferred_element_type=F32)
        acc_ref[...] += gate_col * picked
        return carry

    lax.fori_loop(0, n, window, 0)

    res = _rms(acc_ref[...], gf_ref[...])

    @pl.when(b < prompt_blocks)
    def _():
        op_ref[...] = res

    @pl.when(b >= prompt_blocks)
    def _():
        os_ref[...] = res


def _combine(h, y, cols, plan, g_final, n_prompt):
    t, d = h.shape
    nwin, wstart, wexp, region = plan
    nb = t // ROW_TILE
    npb = n_prompt // ROW_TILE
    return pl.pallas_call(
        functools.partial(_combine_body, prompt_blocks=npb, max_windows=wstart.shape[0] // nb),
        grid_spec=pltpu.PrefetchScalarGridSpec(
            num_scalar_prefetch=4,
            grid=(nb,),
            in_specs=[pl.BlockSpec((ROW_TILE, cols.shape[1]), lambda b, *_: (b, 0)),
                      pl.BlockSpec(memory_space=pl.ANY),
                      pl.BlockSpec((ROW_TILE, d), lambda b, *_: (b, 0)),
                      pl.BlockSpec((1, d), lambda b, *_: (0, 0))],
            out_specs=[pl.BlockSpec((ROW_TILE, d), lambda b, *_: (jnp.minimum(b, npb - 1), 0)),
                       pl.BlockSpec((ROW_TILE, d), lambda b, *_: (jnp.maximum(b - npb, 0), 0))],
            scratch_shapes=[pltpu.VMEM((ROW_TILE, d), F32),
                            pltpu.VMEM((DMA_SLOTS, WINDOW_ROWS, d), BF16),
                            pltpu.SemaphoreType.DMA((DMA_SLOTS,))]),
        out_shape=[jax.ShapeDtypeStruct((n_prompt, d), F32),
                   jax.ShapeDtypeStruct((t - n_prompt, d), F32)],
        compiler_params=_params(("arbitrary",)),
        name="moe_combine",
    )(nwin, wstart, wexp, region, cols, y, h, g_final[None, :])


def _owner(cum, q):
    return jnp.sum(cum[None, :] <= q[:, None], axis=1).astype(I32)


def _moe_plan(cin, cnt, n_tok):
    nb = n_tok // ROW_TILE
    n_slots = n_tok * TOP_K
    s_max = n_slots // MOE_TILE + N_EXPERTS
    counts = cnt[:, 0]
    cblk = jnp.concatenate([cin[:, :, 0].T, counts[:, None]], axis=1)
    ntile = (counts + MOE_TILE - 1) // MOE_TILE
    tcum = jnp.cumsum(ntile)
    tstart = tcum - ntile
    total = tcum[-1]
    s_idx = jnp.arange(s_max, dtype=I32)
    tile_ok = s_idx < total
    tile_e = _owner(tcum, jnp.minimum(s_idx, total - 1))
    tile_k = s_idx - tstart[tile_e]
    tile_rows = jnp.where(tile_ok, jnp.clip(counts[tile_e] - tile_k * MOE_TILE, 0, MOE_TILE), 0)
    tile_nsub = ((tile_rows + SUB_ROWS - 1) // SUB_ROWS).astype(I32)
    region = (tstart * MOE_TILE).astype(I32)

    rel = tile_k * MOE_TILE
    cb = cblk[tile_e]
    b_lo = jnp.clip(jnp.sum(cb[:, 1:] <= rel[:, None], axis=1), 0, nb - 1)
    b_hi = jnp.sum(cb[:, :-1] < (rel + tile_rows)[:, None], axis=1) - 1
    b_hi = jnp.clip(jnp.maximum(b_hi, b_lo), 0, nb - 1)
    fplan = (tile_e, tile_nsub, tile_k.astype(I32), b_lo.astype(I32), b_hi.astype(I32),
             cblk.astype(I32))

    span = (cblk[:, 1:] - cblk[:, :-1]).T
    start = (region[:, None] + cblk[:, :-1]).T
    first = (start // BF16_SUBLANES) * BF16_SUBLANES
    nwin = jnp.where(span > 0, (start + span - first + WINDOW_ROWS - 1) // WINDOW_ROWS, 0)
    wcum = jnp.cumsum(nwin, axis=1)
    wfirst = wcum - nwin
    max_windows = N_EXPERTS * (ROW_TILE // WINDOW_ROWS + 1)
    q = jnp.arange(max_windows, dtype=I32)[None, :]
    q = jnp.minimum(q, wcum[:, -1:] - 1)
    wexp = jnp.sum(wcum[:, None, :] <= q[:, :, None], axis=2).astype(I32)
    k = q - jnp.take_along_axis(wfirst, wexp, axis=1)
    wstart = jnp.take_along_axis(first, wexp, axis=1) + k * WINDOW_ROWS
    cplan = (wcum[:, -1].astype(I32), wstart.reshape(-1).astype(I32), wexp.reshape(-1), region)
    return fplan, cplan, total


def kernel(x_prompt, x_sample, state_pool, g_mix, w_in, g_v, w_pool, pool_scale, w_s, b_s, w_out,
           g_ffn, dense_w_gate, dense_w_up, dense_w_down, w_router, moe_w_gate, moe_w_up,
           moe_w_down, g_final):
    batch, seq, d = x_prompt.shape
    dec_batch, dec_seq, _ = x_sample.shape
    depth = w_in.shape[0]
    n_p = batch * seq
    n_s = dec_batch * dec_seq
    n_tok = n_p + n_s
    assert dec_seq == 8 and depth == 2 and n_p % ROW_TILE == 0 and n_s % ROW_TILE == 0

    addends = ((x_prompt.reshape(n_p, d), x_sample.reshape(n_s, d)),)
    dn = w_in.shape[2] // 3
    hd = dn // N_HEADS
    per_chunk = CHUNK // dec_seq
    pools_p, pools_s, vs_s = [], [], []
    out = None
    w_in_bf16 = w_in.astype(BF16)
    w_out_bf16 = w_out.astype(BF16)
    for i in range(depth):
        p, u, v, v_sample = _inproj(i, addends, g_mix, w_in_bf16, g_v, n_p)
        bias_p = jnp.repeat(b_s[i].T, hd, axis=1)
        ws_s = w_s[i][:, :dec_seq, :dec_seq]
        eye = jnp.eye(per_chunk, dtype=F32)
        wm_s = (eye[None, :, None, :, None] * ws_s[:, None, :, None, :]).reshape(N_HEADS, CHUNK, CHUNK)
        bias_s = jnp.tile(jnp.repeat(b_s[i][:, :dec_seq].T, hd, axis=1), (per_chunk, 1))
        mix_p = _mixer(i, p, u, v, 0, n_p, seq, 0, w_s[i], bias_p, w_pool, pool_scale)
        mix_s = _mixer(i, p, u, v, n_p, n_s, dec_seq, PAST_LEN, wm_s, bias_s, w_pool, pool_scale,
                       state=state_pool)
        dense = i % 2 == 0
        h, *xn = _outproj(i, (mix_p, mix_s), w_out_bf16, addends, g_next=g_ffn if dense else None)

        pools_p.append(jnp.stack([p[(b + 1) * seq - POOL_BUF:(b + 1) * seq] for b in range(batch)]))
        pools_s.append(jnp.concatenate(
            [state_pool[i, :, dec_seq:], p[n_p:].reshape(dec_batch, dec_seq, dn)], axis=1))
        vs_s.append(v_sample.reshape(dec_batch, dec_seq, dn))

        j = i // 2
        if dense:
            y = _dense_ffn(j, xn[0], dense_w_gate, dense_w_up, dense_w_down)
            addends = ((h,), (y,))
        else:
            xn, rank, cols, cin, cnt = _router(i, h, g_ffn, w_router[j].T)
            fplan, cplan, live_tiles = _moe_plan(cin, cnt, n_tok)
            y = lax.cond(
                live_tiles <= N_EXPERTS,
                lambda *ops: _moe_ffn(j, *ops, n_tiles=N_EXPERTS),
                lambda *ops: _moe_ffn(j, *ops, n_tiles=fplan[0].shape[0]),
                xn, rank, fplan, moe_w_gate, moe_w_up, moe_w_down)
            out = _combine(h, y, cols, cplan, g_final, n_p)
            addends = (tuple(out),)

    y_prompt = out[0].reshape(batch, seq, d)
    y_sample = out[1].reshape(dec_batch, dec_seq, d)
    return (y_prompt, y_sample, jnp.stack(pools_p), jnp.stack(pools_s), jnp.stack(vs_s))
```

```python
import functools

import jax
import jax.numpy as jnp
from jax import lax
from jax.experimental import pallas as pl
from jax.experimental.pallas import tpu as pltpu

F32 = jnp.float32
BF16 = jnp.bfloat16
I32 = jnp.int32

EPS = 1e-6
POOL_WINDOWS = (2, 4, 8, 16)
POOL_BUF = max(POOL_WINDOWS) - 1
HALO = 16
CHUNK = 128
N_HEADS = 8
N_EXPERTS = 8
TOP_K = 2
PAST_LEN = 16384

V7X_VMEM_LIMIT = 56 * 1024 * 1024
V7X_MXU_COLS = 256

ROW_TILE = 512
FF_TILE = 256
SUB_ROWS = 256
BLOCK_SUBS = 4
MOE_TILE = 2560
WINDOW_ROWS = 256
BF16_SUBLANES = 16
DMA_SLOTS = 4
DENSE_TILE = 2304


def _params(sem):
    return pltpu.CompilerParams(dimension_semantics=sem, vmem_limit_bytes=V7X_VMEM_LIMIT)


def _rms(x, g):
    return x * lax.rsqrt(jnp.mean(x * x, axis=-1, keepdims=True) + EPS) * g


def _gelu(x):
    return 0.5 * x * (1.0 + lax.erf(x * (2.0 ** -0.5)))


def _stack_offsets(stack):
    offs, o = [], 0
    for a in stack:
        offs.append(o)
        o += a.shape[0] // ROW_TILE
    return tuple(offs), o


def _stack_specs(stack, cols, m_axis, col_fn=None):
    offs, _ = _stack_offsets(stack)
    specs = []
    for a, off in zip(stack, offs):
        nblk = a.shape[0] // ROW_TILE

        def imap(*idx, off=off, nblk=nblk):
            col = 0 if col_fn is None else col_fn(*idx)
            return (jnp.clip(idx[m_axis] - off, 0, nblk - 1), col)

        specs.append(pl.BlockSpec((ROW_TILE, cols), imap))
    return specs


def _stack_read(refs, offs, m):
    v = refs[0][...]
    for ref, off in zip(refs[1:], offs[1:]):
        v = jnp.where(m >= off, ref[...], v)
    return v


def _sum_read(refs, layout, m):
    total, k = None, 0
    for offs in layout:
        part = _stack_read(refs[k:k + len(offs)], offs, m)
        k += len(offs)
        total = part if total is None else total + part
    return total


def _inproj_body(*refs, layout, sample_block0):
    n_x = sum(len(offs) for offs in layout)
    g_ref, w_ref, gv_ref, p_ref, u_ref, v_ref, vs_ref, vtmp_ref = refs[n_x:]
    m = pl.program_id(0)
    x = _sum_read(refs[:n_x], layout, m)
    rowscale = lax.rsqrt(jnp.mean(x * x, axis=-1, keepdims=True) + EPS)
    xg = (x * g_ref[...]).astype(BF16)
    dn = p_ref.shape[1]
    ss = None
    for k, o_ref in enumerate((p_ref, u_ref, vtmp_ref)):
        for c0 in range(0, dn, V7X_MXU_COLS):
            w = w_ref[:, k * dn + c0:k * dn + c0 + V7X_MXU_COLS]
            acc = jnp.dot(xg, w, preferred_element_type=F32) * rowscale
            if o_ref is not p_ref:
                acc = _gelu(acc)
            if o_ref is vtmp_ref:
                part = jnp.sum(acc * acc, axis=1, keepdims=True)
                ss = part if ss is None else ss + part
            o_ref[:, c0:c0 + V7X_MXU_COLS] = acc.astype(o_ref.dtype)
    vn = vtmp_ref[...] * lax.rsqrt(ss / dn + EPS) * gv_ref[...]
    v_ref[...] = vn.astype(BF16)

    @pl.when(m >= sample_block0)
    def _():
        vs_ref[...] = vn


def _inproj(layer, addends, g_mix, w_in_bf16, g_v, n_prompt):
    d = addends[0][0].shape[1]
    dn = w_in_bf16.shape[2] // 3
    layout = tuple(_stack_offsets(st)[0] for st in addends)
    n_blk = _stack_offsets(addends[0])[1]
    x_specs = [sp for st in addends for sp in _stack_specs(st, d, 0)]
    out_spec = pl.BlockSpec((ROW_TILE, dn), lambda m: (m, 0))
    n_tok = n_blk * ROW_TILE
    npb = n_prompt // ROW_TILE
    return pl.pallas_call(
        functools.partial(_inproj_body, layout=layout, sample_block0=npb),
        grid=(n_blk,),
        in_specs=x_specs + [
            pl.BlockSpec((None, 1, d), lambda m: (layer, 0, 0)),
            pl.BlockSpec((None, d, 3 * dn), lambda m: (layer, 0, 0), pipeline_mode=pl.Buffered(1)),
            pl.BlockSpec((None, 1, dn), lambda m: (layer, 0, 0)),
        ],
        out_specs=[out_spec] * 3 + [
            pl.BlockSpec((ROW_TILE, dn), lambda m: (jnp.maximum(m - npb, 0), 0))],
        out_shape=[jax.ShapeDtypeStruct((n_tok, dn), F32),
                   jax.ShapeDtypeStruct((n_tok, dn), BF16),
                   jax.ShapeDtypeStruct((n_tok, dn), BF16),
                   jax.ShapeDtypeStruct((n_tok - n_prompt, dn), F32)],
        scratch_shapes=[pltpu.VMEM((ROW_TILE, dn), F32)],
        compiler_params=_params(("arbitrary",)),
        name=f"inproj{layer}",
    )(*[a for st in addends for a in st], g_mix[:, None, :], w_in_bf16, g_v[:, None, :])


def _mixer_body(*refs, rows, blocks_per_seq, start_pos, sample):
    p_ref, hist_ref, u_ref, v_ref, wm_ref, bias_ref, wp_ref, ps_ref, o_ref, full_ref = refs
    d_pool = p_ref.shape[1]
    gd = d_pool // len(POOL_WINDOWS)
    hd = u_ref.shape[1] // N_HEADS
    blk = pl.program_id(0) % blocks_per_seq
    p = p_ref[...]

    if sample:
        seqs = rows // 8
        for k in range(POOL_BUF):
            full_ref[:, 1 + k, :] = hist_ref[k]
        full_ref[:, HALO:HALO + 8, :] = p.reshape(seqs, 8, d_pool)
    else:
        full_ref[0:HALO, :] = jnp.where(blk == 0, 0.0, hist_ref[...])
        full_ref[HALO:HALO + rows, :] = p
    row_id = lax.broadcasted_iota(I32, (rows, 1), 0)
    pos = start_pos + (row_id % 8 if sample else blk * rows + row_id)
    for gi, w in enumerate(POOL_WINDOWS):
        c0 = gi * gd
        s = None
        for j in range(w):
            if sample:
                term = full_ref[:, HALO - j:HALO - j + 8, c0:c0 + gd].reshape(rows, gd)
            else:
                term = full_ref[HALO - j:HALO - j + rows, c0:c0 + gd]
            s = term if s is None else s + term
        cnt = jnp.minimum(pos + 1, w).astype(F32)
        r = s / cnt - p[:, c0:c0 + gd]
        y = jnp.dot(r.astype(BF16), wp_ref[gi].astype(BF16), preferred_element_type=F32)
        o_ref[:, c0:c0 + gd] = (y * ps_ref[:, c0:c0 + gd]).astype(BF16)

    row = lax.broadcasted_iota(I32, (CHUNK, CHUNK), 0)
    col = lax.broadcasted_iota(I32, (CHUNK, CHUNK), 1)
    for h in range(N_HEADS):
        c0 = h * hd
        wm = jnp.where(row >= col, wm_ref[h], 0.0).astype(BF16)
        for c in range(rows // CHUNK):
            r0 = c * CHUNK
            z = jnp.dot(wm, v_ref[r0:r0 + CHUNK, c0:c0 + hd].astype(BF16),
                        preferred_element_type=F32) + bias_ref[:, c0:c0 + hd]
            o_ref[r0:r0 + CHUNK, d_pool + c0:d_pool + c0 + hd] = (
                u_ref[r0:r0 + CHUNK, c0:c0 + hd] * z).astype(BF16)


def _mixer(layer, p, u, v, row0, n_rows, seq_len, start_pos, wm, bias, w_pool, pool_scale,
           state=None):
    dn = p.shape[1]
    sample = state is not None
    rows = ROW_TILE
    blocks_per_seq = 1 if sample else seq_len // rows
    b0 = row0 // rows
    blk_spec = pl.BlockSpec((rows, dn), lambda i: (b0 + i, 0))
    if sample:
        hist_spec = pl.BlockSpec((None, POOL_BUF, rows // 8, dn), lambda i: (layer, 0, i, 0))
        hist = state
        scratch = pltpu.VMEM((rows // 8, HALO + 8, dn), F32)
    else:
        per = rows // HALO
        hist_spec = pl.BlockSpec((HALO, dn), lambda i: (jnp.maximum((b0 + i) * per - 1, 0), 0))
        hist = p
        scratch = pltpu.VMEM((HALO + rows, dn), F32)
    return pl.pallas_call(
        functools.partial(_mixer_body, rows=rows, blocks_per_seq=blocks_per_seq,
                          start_pos=start_pos, sample=sample),
        grid=(n_rows // rows,),
        in_specs=[
            blk_spec, hist_spec, blk_spec, blk_spec,
            pl.BlockSpec((N_HEADS, CHUNK, CHUNK), lambda i: (0, 0, 0)),
            pl.BlockSpec((CHUNK, dn), lambda i: (0, 0)),
            pl.BlockSpec((None,) + w_pool.shape[1:], lambda i: (layer, 0, 0, 0)),
            pl.BlockSpec((None, 1, dn), lambda i: (layer, 0, 0)),
        ],
        out_specs=pl.BlockSpec((rows, 2 * dn), lambda i: (i, 0)),
        out_shape=jax.ShapeDtypeStruct((n_rows, 2 * dn), BF16),
        scratch_shapes=[scratch],
        compiler_params=_params(("arbitrary",)),
        name=f"mixer{layer}_{'sample' if sample else 'prompt'}",
    )(p, hist, u, v, wm, bias, w_pool, pool_scale[:, None, :])


def _outproj_body(*refs, mix_offs, layout, with_norm):
    n_mix = len(mix_offs)
    n_h = sum(len(offs) for offs in layout)
    w_ref = refs[n_mix]
    h_refs = refs[n_mix + 1:n_mix + 1 + n_h]
    if with_norm:
        g_ref, o_ref, xn_ref = refs[n_mix + 1 + n_h:]
    else:
        o_ref, = refs[n_mix + 1 + n_h:]
    m = pl.program_id(0)
    a = _stack_read(refs[:n_mix], mix_offs, m)
    h = _sum_read(h_refs, layout, m)
    d = o_ref.shape[1]
    ss = None
    for c0 in range(0, d, V7X_MXU_COLS):
        cols = slice(c0, c0 + V7X_MXU_COLS)
        hn = h[:, cols] + jnp.dot(a, w_ref[:, cols], preferred_element_type=F32)
        o_ref[:, cols] = hn
        if with_norm:
            part = jnp.sum(hn * hn, axis=1, keepdims=True)
            ss = part if ss is None else ss + part
    if with_norm:
        xn_ref[...] = (o_ref[...] * lax.rsqrt(ss / d + EPS) * g_ref[...]).astype(BF16)


def _outproj(layer, mix, w_out_bf16, addends, g_next=None):
    k = mix[0].shape[1]
    d = w_out_bf16.shape[2]
    mix_offs, n_blk = _stack_offsets(mix)
    layout = tuple(_stack_offsets(st)[0] for st in addends)
    with_norm = g_next is not None
    row_spec = pl.BlockSpec((ROW_TILE, d), lambda m: (m, 0))
    in_specs = _stack_specs(mix, k, 0) + [
        pl.BlockSpec((None, k, d), lambda m: (layer, 0, 0), pipeline_mode=pl.Buffered(1)),
    ] + [sp for st in addends for sp in _stack_specs(st, d, 0)]
    operands = [*mix, w_out_bf16, *[a for st in addends for a in st]]
    out_specs = [row_spec]
    out_shape = [jax.ShapeDtypeStruct((n_blk * ROW_TILE, d), F32)]
    if with_norm:
        in_specs.append(pl.BlockSpec((None, 1, d), lambda m: (layer, 0, 0)))
        operands.append(g_next[:, None, :])
        out_specs.append(row_spec)
        out_shape.append(jax.ShapeDtypeStruct((n_blk * ROW_TILE, d), BF16))
    return pl.pallas_call(
        functools.partial(_outproj_body, mix_offs=mix_offs, layout=layout, with_norm=with_norm),
        grid=(n_blk,),
        in_specs=in_specs,
        out_specs=out_specs,
        out_shape=out_shape,
        compiler_params=_params(("arbitrary",)),
        name=f"outproj{layer}",
    )(*operands)


def _split_bf16(x, pieces):
    out = []
    for _ in range(pieces):
        term = x.astype(BF16)
        out.append(term)
        x = x - term.astype(F32)
    return out


def _dot_nt(a, b):
    return lax.dot_general(a, b, (((1,), (1,)), ((), ())), preferred_element_type=F32)


def _router_body(h_ref, g_ref, wr_ref, xn_ref, rank_ref, col_ref, cin_ref, cnt_ref,
                 carry_ref, upper_ref, ident_ref):
    tb = h_ref.shape[0]

    @pl.when(pl.program_id(0) == 0)
    def _():
        carry_ref[...] = jnp.zeros_like(carry_ref)
        row = lax.broadcasted_iota(I32, (tb, tb), 0)
        col = lax.broadcasted_iota(I32, (tb, tb), 1)
        upper_ref[...] = jnp.where(row < col, 1.0, 0.0).astype(BF16)
        ident_ref[...] = jnp.where(row == col, 1.0, 0.0).astype(BF16)

    xn = _rms(h_ref[...], g_ref[...])
    x_hi, x_lo = _split_bf16(xn, 2)
    xn_ref[...] = x_hi
    w_hi, w_lo = _split_bf16(wr_ref[...], 2)
    logits = _dot_nt(w_hi, x_hi) + (_dot_nt(w_hi, x_lo) + _dot_nt(w_lo, x_hi))
    eidx = lax.broadcasted_iota(I32, logits.shape, 0)
    m1 = jnp.max(logits, axis=0, keepdims=True)
    i1 = jnp.min(jnp.where(logits == m1, eidx, N_EXPERTS), axis=0, keepdims=True)
    sel1 = eidx == i1
    rest = jnp.where(sel1, -jnp.inf, logits)
    m2 = jnp.max(rest, axis=0, keepdims=True)
    i2 = jnp.min(jnp.where(rest == m2, eidx, N_EXPERTS), axis=0, keepdims=True)
    sel2 = eidx == i2
    e2 = jnp.exp(m2 - m1)
    g1 = 1.0 / (1.0 + e2)
    g2 = e2 / (1.0 + e2)
    gate = jnp.where(sel1, g1, jnp.where(sel2, g2, 0.0))
    routed = jnp.where(sel1, 1.0, jnp.where(sel2, 1.0, 0.0))
    excl = jnp.dot(routed.astype(BF16), upper_ref[...], preferred_element_type=F32)
    carry = carry_ref[...]
    rank = jnp.where(routed > 0.0, excl + carry[:, 0:1], -1.0)
    rank_ref[...] = rank.astype(I32)
    ident = ident_ref[...]
    d_hi, d_mid, d_lo = _split_bf16(jnp.concatenate([rank, gate], axis=0), 3)
    col_ref[...] = (_dot_nt(ident, d_hi) + _dot_nt(ident, d_mid)) + _dot_nt(ident, d_lo)
    cin_ref[...] = carry.astype(I32)
    new_carry = carry + jnp.sum(routed, axis=1, keepdims=True)
    carry_ref[...] = new_carry
    cnt_ref[...] = new_carry.astype(I32)


def _router(layer, h, g, w_router_t):
    t, d = h.shape
    nb = t // ROW_TILE
    e = w_router_t.shape[0]
    return pl.pallas_call(
        _router_body,
        grid=(nb,),
        in_specs=[pl.BlockSpec((ROW_TILE, d), lambda i: (i, 0)),
                  pl.BlockSpec((None, 1, d), lambda i: (layer, 0, 0)),
                  pl.BlockSpec((e, d), lambda i: (0, 0))],
        out_specs=[pl.BlockSpec((ROW_TILE, d), lambda i: (i, 0)),
                   pl.BlockSpec((e, ROW_TILE), lambda i: (0, i)),
                   pl.BlockSpec((ROW_TILE, 2 * e), lambda i: (i, 0)),
                   pl.BlockSpec((None, e, 128), lambda i: (i, 0, 0)),
                   pl.BlockSpec((e, 128), lambda i: (0, 0))],
        out_shape=[jax.ShapeDtypeStruct((t, d), BF16),
                   jax.ShapeDtypeStruct((e, t), I32),
                   jax.ShapeDtypeStruct((t, 2 * e), F32),
                   jax.ShapeDtypeStruct((nb, e, 128), I32),
                   jax.ShapeDtypeStruct((e, 128), I32)],
        scratch_shapes=[pltpu.VMEM((e, 128), F32), pltpu.VMEM((ROW_TILE, ROW_TILE), BF16),
                        pltpu.VMEM((ROW_TILE, ROW_TILE), BF16)],
        compiler_params=_params(("arbitrary",)),
        name=f"router{layer}",
    )(h, g[:, None, :], w_router_t)


def _swiglu_blocks(x_ref, wg_ref, wu_ref, wd_ref, acc_ref, nsub):
    def rows_block(first_sub, n_sub):
        n_rows = n_sub * SUB_ROWS
        r0 = first_sub * SUB_ROWS
        if not isinstance(r0, int):
            r0 = pl.multiple_of(r0, SUB_ROWS)
        x = x_ref[pl.ds(r0, n_rows), :]
        g = jnp.dot(x, wg_ref[...].astype(BF16), preferred_element_type=F32)
        u = jnp.dot(x, wu_ref[...].astype(BF16), preferred_element_type=F32)
        a = (g * jax.nn.sigmoid(g) * u).astype(BF16)
        acc_ref[pl.ds(r0, n_rows), :] += jnp.dot(a, wd_ref[...].astype(BF16),
                                                 preferred_element_type=F32)

    big = BLOCK_SUBS
    if isinstance(nsub, int):
        n_blocks = max(1, nsub // big)
        first = 0
        for m in range(n_blocks):
            size = nsub // n_blocks + (1 if m < nsub % n_blocks else 0)
            rows_block(first, size)
            first += size
        return

    def full_block(m, carry):
        rows_block(m * big, big)
        return carry

    lax.fori_loop(0, nsub // big, full_block, 0)

    for rest in range(1, big):
        @pl.when(nsub % big == rest)
        def _():
            rows_block((nsub // big) * big, rest)


def _dense_ffn_body(x_ref, wg_ref, wu_ref, wd_ref, o_ref):
    @pl.when(pl.program_id(1) == 0)
    def _():
        o_ref[...] = jnp.zeros_like(o_ref)

    _swiglu_blocks(x_ref, wg_ref, wu_ref, wd_ref, o_ref, x_ref.shape[0] // SUB_ROWS)


def _dense_ffn(layer, x, wg, wu, wd):
    rows, d = x.shape
    nf = wg.shape[2] // FF_TILE
    return pl.pallas_call(
        _dense_ffn_body,
        grid=(rows // DENSE_TILE, nf),
        in_specs=[pl.BlockSpec((DENSE_TILE, d), lambda s, f: (s, 0), pipeline_mode=pl.Buffered(1)),
                  pl.BlockSpec((None, d, FF_TILE), lambda s, f: (layer, 0, f)),
                  pl.BlockSpec((None, d, FF_TILE), lambda s, f: (layer, 0, f)),
                  pl.BlockSpec((None, FF_TILE, d), lambda s, f: (layer, f, 0))],
        out_specs=pl.BlockSpec((DENSE_TILE, d), lambda s, f: (s, 0), pipeline_mode=pl.Buffered(1)),
        out_shape=jax.ShapeDtypeStruct((rows, d), F32),
        compiler_params=_params(("arbitrary", "arbitrary")),
        name=f"ffn_dense{layer}",
    )(x, wg, wu, wd)


def _moe_ffn_body(te_ref, ns_ref, tk_ref, blo_ref, bhi_ref, cb_ref,
                  rank_ref, xn_hbm, wg_ref, wu_ref, wd_ref, y_hbm,
                  xs_ref, acc_ref, xbuf_ref, in_sem, out_sem, *, n_blocks):
    s = pl.program_id(0)
    f = pl.program_id(1)
    nsub = ns_ref[s]
    e = te_ref[s]
    tile_rows = xs_ref.shape[0]
    tb = xbuf_ref.shape[1]
    rel0 = tk_ref[s] * tile_rows

    def fetch(b, slot):
        src = xn_hbm.at[pl.ds(pl.multiple_of(b * tb, tb), tb), :]
        return pltpu.make_async_copy(src, xbuf_ref.at[slot], in_sem.at[slot])

    @pl.when((f == 0) & (nsub > 0))
    def _():
        blo = blo_ref[s]
        bhi = bhi_ref[s]
        for ahead in range(DMA_SLOTS - 1):
            @pl.when(blo + ahead <= bhi)
            def _():
                fetch(blo + ahead, ahead).start()

        acc_ref[...] = jnp.zeros_like(acc_ref)
        xs_ref[...] = jnp.zeros_like(xs_ref)

        def token_block(b, carry):
            slot = (b - blo) % DMA_SLOTS
            fetch(b, slot).wait()
            nxt = b + (DMA_SLOTS - 1)

            @pl.when(nxt <= bhi)
            def _():
                fetch(nxt, (nxt - blo) % DMA_SLOTS).start()

            c0 = cb_ref[e * (n_blocks + 1) + b]
            c1 = cb_ref[e * (n_blocks + 1) + b + 1]
            j_lo = jnp.clip((c0 - rel0) // SUB_ROWS, 0, nsub - 1)
            j_hi = jnp.clip((c1 - 1 - rel0) // SUB_ROWS, 0, nsub - 1)
            rank_row = rank_ref[pl.ds(e, 1), pl.ds(pl.multiple_of(b * tb, tb), tb)]
            xb = xbuf_ref.at[slot]

            def sub_block(j, c):
                r0 = pl.multiple_of(j * SUB_ROWS, SUB_ROWS)
                want = lax.broadcasted_iota(I32, (SUB_ROWS, tb), 0) + (rel0 + r0)
                onehot = jnp.where(want == rank_row, 1.0, 0.0).astype(BF16)
                xs_ref[pl.ds(r0, SUB_ROWS), :] += jnp.dot(
                    onehot, xb[...], preferred_element_type=F32).astype(BF16)
                return c

            lax.fori_loop(j_lo, j_hi + 1, sub_block, 0)
            return carry

        lax.fori_loop(blo, bhi + 1, token_block, 0)

    _swiglu_blocks(xs_ref, wg_ref, wu_ref, wd_ref, acc_ref, nsub)

    @pl.when(f == pl.num_programs(1) - 1)
    def _():
        @pl.when(nsub > 0)
        def _():
            xs_ref[...] = acc_ref[...].astype(BF16)

        @pl.when(nsub == 0)
        def _():
            xs_ref[...] = jnp.zeros_like(xs_ref)

        dst = y_hbm.at[pl.ds(pl.multiple_of(s * tile_rows, tile_rows), tile_rows), :]
        out = pltpu.make_async_copy(xs_ref, dst, out_sem.at[0])
        out.start()
        out.wait()

        @pl.when(s == pl.num_programs(0) - 1)
        def _():
            xs_ref[0:WINDOW_ROWS, :] = jnp.zeros((WINDOW_ROWS, xs_ref.shape[1]), BF16)
            end = y_hbm.shape[0] - WINDOW_ROWS
            tail = pltpu.make_async_copy(xs_ref.at[pl.ds(0, WINDOW_ROWS), :],
                                         y_hbm.at[pl.ds(end, WINDOW_ROWS), :], out_sem.at[0])
            tail.start()
            tail.wait()


def _moe_ffn(layer, xn, rank, plan, wg, wu, wd):
    tile_e, tile_nsub, tile_k, tile_blo, tile_bhi, cblk = plan
    t, d = xn.shape
    ff = wg.shape[3]
    nf = ff // FF_TILE
    n_tiles = tile_e.shape[0]
    nb = t // ROW_TILE

    def w_col(s, f, te, ns, *_):
        return (layer, te[s], 0, jnp.where(ns[s] > 0, f, nf - 1))

    def w_row(s, f, te, ns, *_):
        return (layer, te[s], jnp.where(ns[s] > 0, f, nf - 1), 0)

    return pl.pallas_call(
        functools.partial(_moe_ffn_body, n_blocks=nb),
        grid_spec=pltpu.PrefetchScalarGridSpec(
            num_scalar_prefetch=6,
            grid=(n_tiles, nf),
            in_specs=[pl.BlockSpec(rank.shape, lambda s, f, *_: (0, 0), pipeline_mode=pl.Buffered(1)),
                      pl.BlockSpec(memory_space=pl.ANY),
                      pl.BlockSpec((None, None, d, FF_TILE), w_col),
                      pl.BlockSpec((None, None, d, FF_TILE), w_col),
                      pl.BlockSpec((None, None, FF_TILE, d), w_row)],
            out_specs=pl.BlockSpec(memory_space=pl.ANY),
            scratch_shapes=[pltpu.VMEM((MOE_TILE, d), BF16),
                            pltpu.VMEM((MOE_TILE, d), F32),
                            pltpu.VMEM((DMA_SLOTS, ROW_TILE, d), BF16),
                            pltpu.SemaphoreType.DMA((DMA_SLOTS,)),
                            pltpu.SemaphoreType.DMA((1,))]),
        out_shape=jax.ShapeDtypeStruct((n_tiles * MOE_TILE + WINDOW_ROWS, d), BF16),
        compiler_params=_params(("arbitrary", "arbitrary")),
        name=f"ffn_moe{layer}",
    )(tile_e, tile_nsub, tile_k, tile_blo, tile_bhi, cblk.reshape(-1), rank, xn, wg, wu, wd)


def _combine_body(nwin_ref, ws_ref, we_ref, region_ref, col_ref, y_hbm, h_ref, gf_ref,
                  op_ref, os_ref, acc_ref, ybuf_ref, sem, *, prompt_blocks, max_windows):
    b = pl.program_id(0)
    tb = h_ref.shape[0]
    n = nwin_ref[b]
    base = b * max_windows

    def fetch(i, slot):
        start = pl.multiple_of(ws_ref[base + i], BF16_SUBLANES)
        return pltpu.make_async_copy(y_hbm.at[pl.ds(start, WINDOW_ROWS), :], ybuf_ref.at[slot],
                                     sem.at[slot])

    for ahead in range(DMA_SLOTS - 1):
        @pl.when(ahead < n)
        def _():
            fetch(ahead, ahead).start()

    acc_ref[...] = h_ref[...]

    def window(i, carry):
        slot = i % DMA_SLOTS
        fetch(i, slot).wait()
        nxt = i + (DMA_SLOTS - 1)

        @pl.when(nxt < n)
        def _():
            fetch(nxt, nxt % DMA_SLOTS).start()

        e = we_ref[base + i]
        cols = col_ref[...]
        lane = lax.broadcasted_iota(I32, cols.shape, 1)
        rank_col = jnp.sum(jnp.where(lane == e, cols, 0.0), axis=1, keepdims=True)
        gate_col = jnp.sum(jnp.where(lane == e + N_EXPERTS, cols, 0.0), axis=1, keepdims=True)
        first_rank = ws_ref[base + i] - region_ref[e]
        want = (lax.broadcasted_iota(I32, (tb, WINDOW_ROWS), 1) + first_rank).astype(F32)
        onehot = jnp.where(want == rank_col, 1.0, 0.0).astype(BF16)
        picked = jnp.dot(onehot, ybuf_ref[slot], preferred_element_type=F32)
        acc_ref[...] += gate_col * picked
        return carry

    lax.fori_loop(0, n, window, 0)

    res = _rms(acc_ref[...], gf_ref[...])

    @pl.when(b < prompt_blocks)
    def _():
        op_ref[...] = res

    @pl.when(b >= prompt_blocks)
    def _():
        os_ref[...] = res


def _combine(h, y, cols, plan, g_final, n_prompt):
    t, d = h.shape
    nwin, wstart, wexp, region = plan
    nb = t // ROW_TILE
    npb = n_prompt // ROW_TILE
    return pl.pallas_call(
        functools.partial(_combine_body, prompt_blocks=npb, max_windows=wstart.shape[0] // nb),
        grid_spec=pltpu.PrefetchScalarGridSpec(
            num_scalar_prefetch=4,
            grid=(nb,),
            in_specs=[pl.BlockSpec((ROW_TILE, cols.shape[1]), lambda b, *_: (b, 0)),
                      pl.BlockSpec(memory_space=pl.ANY),
                      pl.BlockSpec((ROW_TILE, d), lambda b, *_: (b, 0)),
                      pl.BlockSpec((1, d), lambda b, *_: (0, 0))],
            out_specs=[pl.BlockSpec((ROW_TILE, d), lambda b, *_: (jnp.minimum(b, npb - 1), 0)),
                       pl.BlockSpec((ROW_TILE, d), lambda b, *_: (jnp.maximum(b - npb, 0), 0))],
            scratch_shapes=[pltpu.VMEM((ROW_TILE, d), F32),
                            pltpu.VMEM((DMA_SLOTS, WINDOW_ROWS, d), BF16),
                            pltpu.SemaphoreType.DMA((DMA_SLOTS,))]),
        out_shape=[jax.ShapeDtypeStruct((n_prompt, d), F32),
                   jax.ShapeDtypeStruct((t - n_prompt, d), F32)],
        compiler_params=_params(("arbitrary",)),
        name="moe_combine",
    )(nwin, wstart, wexp, region, cols, y, h, g_final[None, :])


def _owner(cum, q):
    return jnp.sum(cum[None, :] <= q[:, None], axis=1).astype(I32)


def _moe_plan(cin, cnt, n_tok):
    nb = n_tok // ROW_TILE
    n_slots = n_tok * TOP_K
    s_max = n_slots // MOE_TILE + N_EXPERTS
    counts = cnt[:, 0]
    cblk = jnp.concatenate([cin[:, :, 0].T, counts[:, None]], axis=1)
    ntile = (counts + MOE_TILE - 1) // MOE_TILE
    tcum = jnp.cumsum(ntile)
    tstart = tcum - ntile
    total = tcum[-1]
    s_idx = jnp.arange(s_max, dtype=I32)
    tile_ok = s_idx < total
    tile_e = _owner(tcum, jnp.minimum(s_idx, total - 1))
    tile_k = s_idx - tstart[tile_e]
    tile_rows = jnp.where(tile_ok, jnp.clip(counts[tile_e] - tile_k * MOE_TILE, 0, MOE_TILE), 0)
    tile_nsub = ((tile_rows + SUB_ROWS - 1) // SUB_ROWS).astype(I32)
    region = (tstart * MOE_TILE).astype(I32)

    rel = tile_k * MOE_TILE
    cb = cblk[tile_e]
    b_lo = jnp.clip(jnp.sum(cb[:, 1:] <= rel[:, None], axis=1), 0, nb - 1)
    b_hi = jnp.sum(cb[:, :-1] < (rel + tile_rows)[:, None], axis=1) - 1
    b_hi = jnp.clip(jnp.maximum(b_hi, b_lo), 0, nb - 1)
    fplan = (tile_e, tile_nsub, tile_k.astype(I32), b_lo.astype(I32), b_hi.astype(I32),
             cblk.astype(I32))

    span = (cblk[:, 1:] - cblk[:, :-1]).T
    start = (region[:, None] + cblk[:, :-1]).T
    first = (start // BF16_SUBLANES) * BF16_SUBLANES
    nwin = jnp.where(span > 0, (start + span - first + WINDOW_ROWS - 1) // WINDOW_ROWS, 0)
    wcum = jnp.cumsum(nwin, axis=1)
    wfirst = wcum - nwin
    max_windows = N_EXPERTS * (ROW_TILE // WINDOW_ROWS + 1)
    q = jnp.arange(max_windows, dtype=I32)[None, :]
    q = jnp.minimum(q, wcum[:, -1:] - 1)
    wexp = jnp.sum(wcum[:, None, :] <= q[:, :, None], axis=2).astype(I32)
    k = q - jnp.take_along_axis(wfirst, wexp, axis=1)
    wstart = jnp.take_along_axis(first, wexp, axis=1) + k * WINDOW_ROWS
    cplan = (wcum[:, -1].astype(I32), wstart.reshape(-1).astype(I32), wexp.reshape(-1), region)
    return fplan, cplan


def kernel(x_prompt, x_sample, state_pool, g_mix, w_in, g_v, w_pool, pool_scale, w_s, b_s, w_out,
           g_ffn, dense_w_gate, dense_w_up, dense_w_down, w_router, moe_w_gate, moe_w_up,
           moe_w_down, g_final):
    batch, seq, d = x_prompt.shape
    dec_batch, dec_seq, _ = x_sample.shape
    depth = w_in.shape[0]
    n_p = batch * seq
    n_s = dec_batch * dec_seq
    n_tok = n_p + n_s
    assert dec_seq == 8 and depth == 2 and n_p % ROW_TILE == 0 and n_s % ROW_TILE == 0

    addends = ((x_prompt.reshape(n_p, d), x_sample.reshape(n_s, d)),)
    dn = w_in.shape[2] // 3
    hd = dn // N_HEADS
    per_chunk = CHUNK // dec_seq
    pools_p, pools_s, vs_s = [], [], []
    out = None
    w_in_bf16 = w_in.astype(BF16)
    w_out_bf16 = w_out.astype(BF16)
    state_t = jnp.swapaxes(state_pool, 1, 2)
    for i in range(depth):
        p, u, v, v_sample = _inproj(i, addends, g_mix, w_in_bf16, g_v, n_p)
        bias_p = jnp.repeat(b_s[i].T, hd, axis=1)
        ws_s = w_s[i][:, :dec_seq, :dec_seq]
        eye = jnp.eye(per_chunk, dtype=F32)
        wm_s = (eye[None, :, None, :, None] * ws_s[:, None, :, None, :]).reshape(N_HEADS, CHUNK, CHUNK)
        bias_s = jnp.tile(jnp.repeat(b_s[i][:, :dec_seq].T, hd, axis=1), (per_chunk, 1))
        mix_p = _mixer(i, p, u, v, 0, n_p, seq, 0, w_s[i], bias_p, w_pool, pool_scale)
        mix_s = _mixer(i, p, u, v, n_p, n_s, dec_seq, PAST_LEN, wm_s, bias_s, w_pool, pool_scale,
                       state=state_t)
        dense = i % 2 == 0
        h, *xn = _outproj(i, (mix_p, mix_s), w_out_bf16, addends, g_next=g_ffn if dense else None)

        pools_p.append(jnp.stack([p[(b + 1) * seq - POOL_BUF:(b + 1) * seq] for b in range(batch)]))
        pools_s.append(jnp.concatenate(
            [state_pool[i, :, dec_seq:], p[n_p:].reshape(dec_batch, dec_seq, dn)], axis=1))
        vs_s.append(v_sample.reshape(dec_batch, dec_seq, dn))

        j = i // 2
        if dense:
            y = _dense_ffn(j, xn[0], dense_w_gate, dense_w_up, dense_w_down)
            addends = ((h,), (y,))
        else:
            xn, rank, cols, cin, cnt = _router(i, h, g_ffn, w_router[j].T)
            fplan, cplan = _moe_plan(cin, cnt, n_tok)
            y = _moe_ffn(j, xn, rank, fplan, moe_w_gate, moe_w_up, moe_w_down)
            out = _combine(h, y, cols, cplan, g_final, n_p)
            addends = (tuple(out),)

    y_prompt = out[0].reshape(batch, seq, d)
    y_sample = out[1].reshape(dec_batch, dec_seq, d)
    return (y_prompt, y_sample, jnp.stack(pools_p), jnp.stack(pools_s), jnp.stack(vs_s))
```

```python
import functools

import jax
import jax.numpy as jnp
from jax import lax
from jax.experimental import pallas as pl
from jax.experimental.pallas import tpu as pltpu

F32 = jnp.float32
BF16 = jnp.bfloat16
I32 = jnp.int32

EPS = 1e-6
POOL_WINDOWS = (2, 4, 8, 16)
POOL_BUF = max(POOL_WINDOWS) - 1
HALO = 16
SHIFT_PAD = 8
CHUNK = 128
N_HEADS = 8
N_EXPERTS = 8
TOP_K = 2
PAST_LEN = 16384

V7X_VMEM_LIMIT = 56 * 1024 * 1024
V7X_MXU_COLS = 256

ROW_TILE = 512
FF_TILE = 256
SUB_ROWS = 256
BLOCK_SUBS = 5
MOE_TILE = 2560
WINDOW_ROWS = 256
BF16_SUBLANES = 16
DMA_SLOTS = 4
DENSE_TILE = 2304


def _params(sem):
    return pltpu.CompilerParams(dimension_semantics=sem, vmem_limit_bytes=V7X_VMEM_LIMIT)


def _rms(x, g):
    return x * lax.rsqrt(jnp.mean(x * x, axis=-1, keepdims=True) + EPS) * g


def _gelu(x):
    return 0.5 * x * (1.0 + lax.erf(x * (2.0 ** -0.5)))


def _stack_offsets(stack):
    offs, o = [], 0
    for a in stack:
        offs.append(o)
        o += a.shape[0] // ROW_TILE
    return tuple(offs), o


def _stack_specs(stack, cols, m_axis, col_fn=None):
    offs, _ = _stack_offsets(stack)
    specs = []
    for a, off in zip(stack, offs):
        nblk = a.shape[0] // ROW_TILE

        def imap(*idx, off=off, nblk=nblk):
            col = 0 if col_fn is None else col_fn(*idx)
            return (jnp.clip(idx[m_axis] - off, 0, nblk - 1), col)

        specs.append(pl.BlockSpec((ROW_TILE, cols), imap))
    return specs


def _stack_read(refs, offs, m):
    v = refs[0][...]
    for ref, off in zip(refs[1:], offs[1:]):
        v = jnp.where(m >= off, ref[...], v)
    return v


def _sum_read(refs, layout, m):
    total, k = None, 0
    for offs in layout:
        part = _stack_read(refs[k:k + len(offs)], offs, m)
        k += len(offs)
        total = part if total is None else total + part
    return total


def _inproj_body(*refs, layout, sample_block0):
    n_x = sum(len(offs) for offs in layout)
    g_ref, w_ref, gv_ref, p_ref, u_ref, v_ref, vs_ref, vtmp_ref = refs[n_x:]
    m = pl.program_id(0)
    x = _sum_read(refs[:n_x], layout, m)
    rowscale = lax.rsqrt(jnp.mean(x * x, axis=-1, keepdims=True) + EPS)
    xg = (x * g_ref[...]).astype(BF16)
    dn = p_ref.shape[1]
    ss = None
    for k, o_ref in enumerate((p_ref, u_ref, vtmp_ref)):
        for c0 in range(0, dn, V7X_MXU_COLS):
            w = w_ref[:, k * dn + c0:k * dn + c0 + V7X_MXU_COLS]
            acc = jnp.dot(xg, w, preferred_element_type=F32) * rowscale
            if o_ref is not p_ref:
                acc = _gelu(acc)
            if o_ref is vtmp_ref:
                part = jnp.sum(acc * acc, axis=1, keepdims=True)
                ss = part if ss is None else ss + part
            o_ref[:, c0:c0 + V7X_MXU_COLS] = acc.astype(o_ref.dtype)
    vn = vtmp_ref[...] * lax.rsqrt(ss / dn + EPS) * gv_ref[...]
    v_ref[...] = vn.astype(BF16)

    @pl.when(m >= sample_block0)
    def _():
        vs_ref[...] = vn


def _inproj(layer, addends, g_mix, w_in_bf16, g_v, n_prompt):
    d = addends[0][0].shape[1]
    dn = w_in_bf16.shape[2] // 3
    layout = tuple(_stack_offsets(st)[0] for st in addends)
    n_blk = _stack_offsets(addends[0])[1]
    x_specs = [sp for st in addends for sp in _stack_specs(st, d, 0)]
    out_spec = pl.BlockSpec((ROW_TILE, dn), lambda m: (m, 0))
    n_tok = n_blk * ROW_TILE
    npb = n_prompt // ROW_TILE
    return pl.pallas_call(
        functools.partial(_inproj_body, layout=layout, sample_block0=npb),
        grid=(n_blk,),
        in_specs=x_specs + [
            pl.BlockSpec((None, 1, d), lambda m: (layer, 0, 0)),
            pl.BlockSpec((None, d, 3 * dn), lambda m: (layer, 0, 0), pipeline_mode=pl.Buffered(1)),
            pl.BlockSpec((None, 1, dn), lambda m: (layer, 0, 0)),
        ],
        out_specs=[out_spec] * 3 + [
            pl.BlockSpec((ROW_TILE, dn), lambda m: (jnp.maximum(m - npb, 0), 0))],
        out_shape=[jax.ShapeDtypeStruct((n_tok, dn), F32),
                   jax.ShapeDtypeStruct((n_tok, dn), BF16),
                   jax.ShapeDtypeStruct((n_tok, dn), BF16),
                   jax.ShapeDtypeStruct((n_tok - n_prompt, dn), F32)],
        scratch_shapes=[pltpu.VMEM((ROW_TILE, dn), F32)],
        compiler_params=_params(("arbitrary",)),
        name=f"inproj{layer}",
    )(*[a for st in addends for a in st], g_mix[:, None, :], w_in_bf16, g_v[:, None, :])


def _mixer_body(*refs, rows, blocks_per_seq, start_pos, sample):
    p_ref, hist_ref, u_ref, v_ref, wm_ref, bias_ref, wp_ref, ps_ref, o_ref, full_ref = refs[:10]
    d_pool = p_ref.shape[1]
    gd = d_pool // len(POOL_WINDOWS)
    hd = u_ref.shape[1] // N_HEADS
    blk = pl.program_id(0) % blocks_per_seq
    p = p_ref[...]

    if sample:
        seqs = rows // 8
        for k in range(POOL_BUF):
            full_ref[:, 1 + k, :] = hist_ref[k]
        full_ref[:, HALO:HALO + 8, :] = p.reshape(seqs, 8, d_pool)
    else:
        level_refs = refs[10:]
        ext = HALO + rows
        zeros = jnp.zeros((SHIFT_PAD, d_pool), F32)
        full_ref[0:SHIFT_PAD, :] = zeros
        full_ref[SHIFT_PAD:SHIFT_PAD + HALO, :] = jnp.where(blk == 0, 0.0, hist_ref[...])
        full_ref[SHIFT_PAD + HALO:SHIFT_PAD + ext, :] = p
        for ref in level_refs:
            ref[0:SHIFT_PAD, :] = zeros[:, :gd]
    row_id = lax.broadcasted_iota(I32, (rows, 1), 0)
    pos = start_pos + (row_id % 8 if sample else blk * rows + row_id)
    for gi, w in enumerate(POOL_WINDOWS):
        c0 = gi * gd
        if sample:
            s = None
            for j in range(w):
                term = full_ref[:, HALO - j:HALO - j + 8, c0:c0 + gd].reshape(rows, gd)
                s = term if s is None else s + term
        else:
            def read(lo, hi, c0=c0):
                return full_ref[lo:hi, c0:c0 + gd]

            shift, level = 1, 0
            while shift < w:
                both = (read(SHIFT_PAD, SHIFT_PAD + ext)
                        + read(SHIFT_PAD - shift, SHIFT_PAD + ext - shift))
                shift *= 2
                if shift < w:
                    dst = level_refs[level % 2]
                    dst[SHIFT_PAD:SHIFT_PAD + ext, :] = both

                    def read(lo, hi, dst=dst):
                        return dst[lo:hi, :]

                    level += 1
            s = both[HALO:, :]
        cnt = jnp.minimum(pos + 1, w).astype(F32)
        r = s / cnt - p[:, c0:c0 + gd]
        y = jnp.dot(r.astype(BF16), wp_ref[gi].astype(BF16), preferred_element_type=F32)
        o_ref[:, c0:c0 + gd] = (y * ps_ref[:, c0:c0 + gd]).astype(BF16)

    row = lax.broadcasted_iota(I32, (CHUNK, CHUNK), 0)
    col = lax.broadcasted_iota(I32, (CHUNK, CHUNK), 1)
    for h in range(N_HEADS):
        c0 = h * hd
        wm = jnp.where(row >= col, wm_ref[h], 0.0).astype(BF16)
        for c in range(rows // CHUNK):
            r0 = c * CHUNK
            z = jnp.dot(wm, v_ref[r0:r0 + CHUNK, c0:c0 + hd].astype(BF16),
                        preferred_element_type=F32) + bias_ref[:, c0:c0 + hd]
            o_ref[r0:r0 + CHUNK, d_pool + c0:d_pool + c0 + hd] = (
                u_ref[r0:r0 + CHUNK, c0:c0 + hd] * z).astype(BF16)


def _mixer(layer, p, u, v, row0, n_rows, seq_len, start_pos, wm, bias, w_pool, pool_scale,
           state=None):
    dn = p.shape[1]
    sample = state is not None
    rows = ROW_TILE
    blocks_per_seq = 1 if sample else seq_len // rows
    b0 = row0 // rows
    blk_spec = pl.BlockSpec((rows, dn), lambda i: (b0 + i, 0))
    if sample:
        hist_spec = pl.BlockSpec((None, POOL_BUF, rows // 8, dn), lambda i: (layer, 0, i, 0))
        hist = state
        scratch = [pltpu.VMEM((rows // 8, HALO + 8, dn), F32)]
    else:
        per = rows // HALO
        hist_spec = pl.BlockSpec((HALO, dn), lambda i: (jnp.maximum((b0 + i) * per - 1, 0), 0))
        hist = p
        ext = SHIFT_PAD + HALO + rows
        gd = dn // len(POOL_WINDOWS)
        scratch = [pltpu.VMEM((ext, dn), F32), pltpu.VMEM((ext, gd), F32),
                   pltpu.VMEM((ext, gd), F32)]
    return pl.pallas_call(
        functools.partial(_mixer_body, rows=rows, blocks_per_seq=blocks_per_seq,
                          start_pos=start_pos, sample=sample),
        grid=(n_rows // rows,),
        in_specs=[
            blk_spec, hist_spec, blk_spec, blk_spec,
            pl.BlockSpec((N_HEADS, CHUNK, CHUNK), lambda i: (0, 0, 0)),
            pl.BlockSpec((CHUNK, dn), lambda i: (0, 0)),
            pl.BlockSpec((None,) + w_pool.shape[1:], lambda i: (layer, 0, 0, 0)),
            pl.BlockSpec((None, 1, dn), lambda i: (layer, 0, 0)),
        ],
        out_specs=pl.BlockSpec((rows, 2 * dn), lambda i: (i, 0)),
        out_shape=jax.ShapeDtypeStruct((n_rows, 2 * dn), BF16),
        scratch_shapes=scratch,
        compiler_params=_params(("arbitrary",)),
        name=f"mixer{layer}_{'sample' if sample else 'prompt'}",
    )(p, hist, u, v, wm, bias, w_pool, pool_scale[:, None, :])


def _outproj_body(*refs, mix_offs, layout, with_norm):
    n_mix = len(mix_offs)
    n_h = sum(len(offs) for offs in layout)
    w_ref = refs[n_mix]
    h_refs = refs[n_mix + 1:n_mix + 1 + n_h]
    if with_norm:
        g_ref, o_ref, xn_ref = refs[n_mix + 1 + n_h:]
    else:
        o_ref, = refs[n_mix + 1 + n_h:]
    m = pl.program_id(0)
    a = _stack_read(refs[:n_mix], mix_offs, m)
    h = _sum_read(h_refs, layout, m)
    d = o_ref.shape[1]
    ss = None
    for c0 in range(0, d, V7X_MXU_COLS):
        cols = slice(c0, c0 + V7X_MXU_COLS)
        hn = h[:, cols] + jnp.dot(a, w_ref[:, cols], preferred_element_type=F32)
        o_ref[:, cols] = hn
        if with_norm:
            part = jnp.sum(hn * hn, axis=1, keepdims=True)
            ss = part if ss is None else ss + part
    if with_norm:
        xn_ref[...] = (o_ref[...] * lax.rsqrt(ss / d + EPS) * g_ref[...]).astype(BF16)


def _outproj(layer, mix, w_out_bf16, addends, g_next=None):
    k = mix[0].shape[1]
    d = w_out_bf16.shape[2]
    mix_offs, n_blk = _stack_offsets(mix)
    layout = tuple(_stack_offsets(st)[0] for st in addends)
    with_norm = g_next is not None
    row_spec = pl.BlockSpec((ROW_TILE, d), lambda m: (m, 0))
    in_specs = _stack_specs(mix, k, 0) + [
        pl.BlockSpec((None, k, d), lambda m: (layer, 0, 0), pipeline_mode=pl.Buffered(1)),
    ] + [sp for st in addends for sp in _stack_specs(st, d, 0)]
    operands = [*mix, w_out_bf16, *[a for st in addends for a in st]]
    out_specs = [row_spec]
    out_shape = [jax.ShapeDtypeStruct((n_blk * ROW_TILE, d), F32)]
    if with_norm:
        in_specs.append(pl.BlockSpec((None, 1, d), lambda m: (layer, 0, 0)))
        operands.append(g_next[:, None, :])
        out_specs.append(row_spec)
        out_shape.append(jax.ShapeDtypeStruct((n_blk * ROW_TILE, d), BF16))
    return pl.pallas_call(
        functools.partial(_outproj_body, mix_offs=mix_offs, layout=layout, with_norm=with_norm),
        grid=(n_blk,),
        in_specs=in_specs,
        out_specs=out_specs,
        out_shape=out_shape,
        compiler_params=_params(("arbitrary",)),
        name=f"outproj{layer}",
    )(*operands)


def _split_bf16(x, pieces):
    out = []
    for _ in range(pieces):
        term = x.astype(BF16)
        out.append(term)
        x = x - term.astype(F32)
    return out


def _dot_nt(a, b):
    return lax.dot_general(a, b, (((1,), (1,)), ((), ())), preferred_element_type=F32)


def _router_body(h_ref, g_ref, wr_ref, xn_ref, rank_ref, col_ref, cin_ref, cnt_ref,
                 carry_ref, upper_ref, ident_ref):
    tb = h_ref.shape[0]

    @pl.when(pl.program_id(0) == 0)
    def _():
        carry_ref[...] = jnp.zeros_like(carry_ref)
        row = lax.broadcasted_iota(I32, (tb, tb), 0)
        col = lax.broadcasted_iota(I32, (tb, tb), 1)
        upper_ref[...] = jnp.where(row < col, 1.0, 0.0).astype(BF16)
        ident_ref[...] = jnp.where(row == col, 1.0, 0.0).astype(BF16)

    xn = _rms(h_ref[...], g_ref[...])
    x_hi, x_lo = _split_bf16(xn, 2)
    xn_ref[...] = x_hi
    w_hi, w_lo = _split_bf16(wr_ref[...], 2)
    logits = _dot_nt(w_hi, x_hi) + (_dot_nt(w_hi, x_lo) + _dot_nt(w_lo, x_hi))
    eidx = lax.broadcasted_iota(I32, logits.shape, 0)
    m1 = jnp.max(logits, axis=0, keepdims=True)
    i1 = jnp.min(jnp.where(logits == m1, eidx, N_EXPERTS), axis=0, keepdims=True)
    sel1 = eidx == i1
    rest = jnp.where(sel1, -jnp.inf, logits)
    m2 = jnp.max(rest, axis=0, keepdims=True)
    i2 = jnp.min(jnp.where(rest == m2, eidx, N_EXPERTS), axis=0, keepdims=True)
    sel2 = eidx == i2
    e2 = jnp.exp(m2 - m1)
    g1 = 1.0 / (1.0 + e2)
    g2 = e2 / (1.0 + e2)
    gate = jnp.where(sel1, g1, jnp.where(sel2, g2, 0.0))
    routed = jnp.where(sel1, 1.0, jnp.where(sel2, 1.0, 0.0))
    excl = jnp.dot(routed.astype(BF16), upper_ref[...], preferred_element_type=F32)
    carry = carry_ref[...]
    rank = jnp.where(routed > 0.0, excl + carry[:, 0:1], -1.0)
    rank_ref[...] = rank.astype(I32)
    ident = ident_ref[...]
    d_hi, d_mid, d_lo = _split_bf16(jnp.concatenate([rank, gate], axis=0), 3)
    col_ref[...] = (_dot_nt(ident, d_hi) + _dot_nt(ident, d_mid)) + _dot_nt(ident, d_lo)
    cin_ref[...] = carry.astype(I32)
    new_carry = carry + jnp.sum(routed, axis=1, keepdims=True)
    carry_ref[...] = new_carry
    cnt_ref[...] = new_carry.astype(I32)


def _router(layer, h, g, w_router_t):
    t, d = h.shape
    nb = t // ROW_TILE
    e = w_router_t.shape[0]
    return pl.pallas_call(
        _router_body,
        grid=(nb,),
        in_specs=[pl.BlockSpec((ROW_TILE, d), lambda i: (i, 0)),
                  pl.BlockSpec((None, 1, d), lambda i: (layer, 0, 0)),
                  pl.BlockSpec((e, d), lambda i: (0, 0))],
        out_specs=[pl.BlockSpec((ROW_TILE, d), lambda i: (i, 0)),
                   pl.BlockSpec((e, ROW_TILE), lambda i: (0, i)),
                   pl.BlockSpec((ROW_TILE, 2 * e), lambda i: (i, 0)),
                   pl.BlockSpec((None, e, 128), lambda i: (i, 0, 0)),
                   pl.BlockSpec((e, 128), lambda i: (0, 0))],
        out_shape=[jax.ShapeDtypeStruct((t, d), BF16),
                   jax.ShapeDtypeStruct((e, t), I32),
                   jax.ShapeDtypeStruct((t, 2 * e), F32),
                   jax.ShapeDtypeStruct((nb, e, 128), I32),
                   jax.ShapeDtypeStruct((e, 128), I32)],
        scratch_shapes=[pltpu.VMEM((e, 128), F32), pltpu.VMEM((ROW_TILE, ROW_TILE), BF16),
                        pltpu.VMEM((ROW_TILE, ROW_TILE), BF16)],
        compiler_params=_params(("arbitrary",)),
        name=f"router{layer}",
    )(h, g[:, None, :], w_router_t)


def _swiglu_blocks(x_ref, wg_ref, wu_ref, wd_ref, acc_ref, nsub):
    def rows_block(first_sub, n_sub):
        n_rows = n_sub * SUB_ROWS
        r0 = first_sub * SUB_ROWS
        if not isinstance(r0, int):
            r0 = pl.multiple_of(r0, SUB_ROWS)
        x = x_ref[pl.ds(r0, n_rows), :]
        g = jnp.dot(x, wg_ref[...].astype(BF16), preferred_element_type=F32)
        u = jnp.dot(x, wu_ref[...].astype(BF16), preferred_element_type=F32)
        a = (g * jax.nn.sigmoid(g) * u).astype(BF16)
        acc_ref[pl.ds(r0, n_rows), :] += jnp.dot(a, wd_ref[...].astype(BF16),
                                                 preferred_element_type=F32)

    big = BLOCK_SUBS
    if isinstance(nsub, int):
        n_blocks = max(1, nsub // big)
        first = 0
        for m in range(n_blocks):
            size = nsub // n_blocks + (1 if m < nsub % n_blocks else 0)
            rows_block(first, size)
            first += size
        return

    def full_block(m, carry):
        rows_block(m * big, big)
        return carry

    lax.fori_loop(0, nsub // big, full_block, 0)

    for rest in range(1, big):
        @pl.when(nsub % big == rest)
        def _():
            rows_block((nsub // big) * big, rest)


def _dense_ffn_body(x_ref, wg_ref, wu_ref, wd_ref, o_ref):
    @pl.when(pl.program_id(1) == 0)
    def _():
        o_ref[...] = jnp.zeros_like(o_ref)

    _swiglu_blocks(x_ref, wg_ref, wu_ref, wd_ref, o_ref, x_ref.shape[0] // SUB_ROWS)


def _dense_ffn(layer, x, wg, wu, wd):
    rows, d = x.shape
    nf = wg.shape[2] // FF_TILE
    return pl.pallas_call(
        _dense_ffn_body,
        grid=(rows // DENSE_TILE, nf),
        in_specs=[pl.BlockSpec((DENSE_TILE, d), lambda s, f: (s, 0), pipeline_mode=pl.Buffered(1)),
                  pl.BlockSpec((None, d, FF_TILE), lambda s, f: (layer, 0, f)),
                  pl.BlockSpec((None, d, FF_TILE), lambda s, f: (layer, 0, f)),
                  pl.BlockSpec((None, FF_TILE, d), lambda s, f: (layer, f, 0))],
        out_specs=pl.BlockSpec((DENSE_TILE, d), lambda s, f: (s, 0), pipeline_mode=pl.Buffered(1)),
        out_shape=jax.ShapeDtypeStruct((rows, d), F32),
        compiler_params=_params(("arbitrary", "arbitrary")),
        name=f"ffn_dense{layer}",
    )(x, wg, wu, wd)


def _moe_ffn_body(te_ref, ns_ref, tk_ref, blo_ref, bhi_ref, cb_ref,
                  rank_ref, xn_hbm, wg_ref, wu_ref, wd_ref, y_hbm,
                  xs_ref, acc_ref, xbuf_ref, in_sem, out_sem, *, n_blocks):
    s = pl.program_id(0)
    f = pl.program_id(1)
    nsub = ns_ref[s]
    e = te_ref[s]
    tile_rows = xs_ref.shape[0]
    tb = xbuf_ref.shape[1]
    rel0 = tk_ref[s] * tile_rows

    def fetch(b, slot):
        src = xn_hbm.at[pl.ds(pl.multiple_of(b * tb, tb), tb), :]
        return pltpu.make_async_copy(src, xbuf_ref.at[slot], in_sem.at[slot])

    @pl.when((f == 0) & (nsub > 0))
    def _():
        blo = blo_ref[s]
        bhi = bhi_ref[s]
        for ahead in range(DMA_SLOTS - 1):
            @pl.when(blo + ahead <= bhi)
            def _():
                fetch(blo + ahead, ahead).start()

        acc_ref[...] = jnp.zeros_like(acc_ref)
        xs_ref[...] = jnp.zeros_like(xs_ref)

        def token_block(b, carry):
            slot = (b - blo) % DMA_SLOTS
            fetch(b, slot).wait()
            nxt = b + (DMA_SLOTS - 1)

            @pl.when(nxt <= bhi)
            def _():
                fetch(nxt, (nxt - blo) % DMA_SLOTS).start()

            c0 = cb_ref[e * (n_blocks + 1) + b]
            c1 = cb_ref[e * (n_blocks + 1) + b + 1]
            j_lo = jnp.clip((c0 - rel0) // SUB_ROWS, 0, nsub - 1)
            j_hi = jnp.clip((c1 - 1 - rel0) // SUB_ROWS, 0, nsub - 1)
            rank_row = rank_ref[pl.ds(e, 1), pl.ds(pl.multiple_of(b * tb, tb), tb)]
            xb = xbuf_ref.at[slot]

            def sub_block(j, c):
                r0 = pl.multiple_of(j * SUB_ROWS, SUB_ROWS)
                want = lax.broadcasted_iota(I32, (SUB_ROWS, tb), 0) + (rel0 + r0)
                onehot = jnp.where(want == rank_row, 1.0, 0.0).astype(BF16)
                xs_ref[pl.ds(r0, SUB_ROWS), :] += jnp.dot(
                    onehot, xb[...], preferred_element_type=F32).astype(BF16)
                return c

            lax.fori_loop(j_lo, j_hi + 1, sub_block, 0)
            return carry

        lax.fori_loop(blo, bhi + 1, token_block, 0)

    _swiglu_blocks(xs_ref, wg_ref, wu_ref, wd_ref, acc_ref, nsub)

    @pl.when(f == pl.num_programs(1) - 1)
    def _():
        @pl.when(nsub > 0)
        def _():
            xs_ref[...] = acc_ref[...].astype(BF16)

        @pl.when(nsub == 0)
        def _():
            xs_ref[...] = jnp.zeros_like(xs_ref)

        dst = y_hbm.at[pl.ds(pl.multiple_of(s * tile_rows, tile_rows), tile_rows), :]
        out = pltpu.make_async_copy(xs_ref, dst, out_sem.at[0])
        out.start()
        out.wait()

        @pl.when(s == pl.num_programs(0) - 1)
        def _():
            xs_ref[0:WINDOW_ROWS, :] = jnp.zeros((WINDOW_ROWS, xs_ref.shape[1]), BF16)
            end = y_hbm.shape[0] - WINDOW_ROWS
            tail = pltpu.make_async_copy(xs_ref.at[pl.ds(0, WINDOW_ROWS), :],
                                         y_hbm.at[pl.ds(end, WINDOW_ROWS), :], out_sem.at[0])
            tail.start()
            tail.wait()


def _moe_ffn(layer, xn, rank, plan, wg, wu, wd):
    tile_e, tile_nsub, tile_k, tile_blo, tile_bhi, cblk = plan
    t, d = xn.shape
    ff = wg.shape[3]
    nf = ff // FF_TILE
    n_tiles = tile_e.shape[0]
    nb = t // ROW_TILE

    def w_col(s, f, te, ns, *_):
        return (layer, te[s], 0, jnp.where(ns[s] > 0, f, nf - 1))

    def w_row(s, f, te, ns, *_):
        return (layer, te[s], jnp.where(ns[s] > 0, f, nf - 1), 0)

    return pl.pallas_call(
        functools.partial(_moe_ffn_body, n_blocks=nb),
        grid_spec=pltpu.PrefetchScalarGridSpec(
            num_scalar_prefetch=6,
            grid=(n_tiles, nf),
            in_specs=[pl.BlockSpec(rank.shape, lambda s, f, *_: (0, 0), pipeline_mode=pl.Buffered(1)),
                      pl.BlockSpec(memory_space=pl.ANY),
                      pl.BlockSpec((None, None, d, FF_TILE), w_col),
                      pl.BlockSpec((None, None, d, FF_TILE), w_col),
                      pl.BlockSpec((None, None, FF_TILE, d), w_row)],
            out_specs=pl.BlockSpec(memory_space=pl.ANY),
            scratch_shapes=[pltpu.VMEM((MOE_TILE, d), BF16),
                            pltpu.VMEM((MOE_TILE, d), F32),
                            pltpu.VMEM((DMA_SLOTS, ROW_TILE, d), BF16),
                            pltpu.SemaphoreType.DMA((DMA_SLOTS,)),
                            pltpu.SemaphoreType.DMA((1,))]),
        out_shape=jax.ShapeDtypeStruct((n_tiles * MOE_TILE + WINDOW_ROWS, d), BF16),
        compiler_params=_params(("arbitrary", "arbitrary")),
        name=f"ffn_moe{layer}",
    )(tile_e, tile_nsub, tile_k, tile_blo, tile_bhi, cblk.reshape(-1), rank, xn, wg, wu, wd)


def _combine_body(nwin_ref, ws_ref, we_ref, region_ref, col_ref, y_hbm, h_ref, gf_ref,
                  op_ref, os_ref, acc_ref, ybuf_ref, sem, *, prompt_blocks, max_windows):
    b = pl.program_id(0)
    tb = h_ref.shape[0]
    n = nwin_ref[b]
    base = b * max_windows

    def fetch(i, slot):
        start = pl.multiple_of(ws_ref[base + i], BF16_SUBLANES)
        return pltpu.make_async_copy(y_hbm.at[pl.ds(start, WINDOW_ROWS), :], ybuf_ref.at[slot],
                                     sem.at[slot])

    for ahead in range(DMA_SLOTS - 1):
        @pl.when(ahead < n)
        def _():
            fetch(ahead, ahead).start()

    acc_ref[...] = h_ref[...]

    def window(i, carry):
        slot = i % DMA_SLOTS
        fetch(i, slot).wait()
        nxt = i + (DMA_SLOTS - 1)

        @pl.when(nxt < n)
        def _():
            fetch(nxt, nxt % DMA_SLOTS).start()

        e = we_ref[base + i]
        cols = col_ref[...]
        lane = lax.broadcasted_iota(I32, cols.shape, 1)
        rank_col = jnp.sum(jnp.where(lane == e, cols, 0.0), axis=1, keepdims=True)
        gate_col = jnp.sum(jnp.where(lane == e + N_EXPERTS, cols, 0.0), axis=1, keepdims=True)
        first_rank = ws_ref[base + i] - region_ref[e]
        want = (lax.broadcasted_iota(I32, (tb, WINDOW_ROWS), 1) + first_rank).astype(F32)
        onehot = jnp.where(want == rank_col, 1.0, 0.0).astype(BF16)
        picked = jnp.dot(onehot, ybuf_ref[slot], preferred_element_type=F32)
        acc_ref[...] += gate_col * picked
        return carry

    lax.fori_loop(0, n, window, 0)

    res = _rms(acc_ref[...], gf_ref[...])

    @pl.when(b < prompt_blocks)
    def _():
        op_ref[...] = res

    @pl.when(b >= prompt_blocks)
    def _():
        os_ref[...] = res


def _combine(h, y, cols, plan, g_final, n_prompt):
    t, d = h.shape
    nwin, wstart, wexp, region = plan
    nb = t // ROW_TILE
    npb = n_prompt // ROW_TILE
    return pl.pallas_call(
        functools.partial(_combine_body, prompt_blocks=npb, max_windows=wstart.shape[0] // nb),
        grid_spec=pltpu.PrefetchScalarGridSpec(
            num_scalar_prefetch=4,
            grid=(nb,),
            in_specs=[pl.BlockSpec((ROW_TILE, cols.shape[1]), lambda b, *_: (b, 0)),
                      pl.BlockSpec(memory_space=pl.ANY),
                      pl.BlockSpec((ROW_TILE, d), lambda b, *_: (b, 0)),
                      pl.BlockSpec((1, d), lambda b, *_: (0, 0))],
            out_specs=[pl.BlockSpec((ROW_TILE, d), lambda b, *_: (jnp.minimum(b, npb - 1), 0)),
                       pl.BlockSpec((ROW_TILE, d), lambda b, *_: (jnp.maximum(b - npb, 0), 0))],
            scratch_shapes=[pltpu.VMEM((ROW_TILE, d), F32),
                            pltpu.VMEM((DMA_SLOTS, WINDOW_ROWS, d), BF16),
                            pltpu.SemaphoreType.DMA((DMA_SLOTS,))]),
        out_shape=[jax.ShapeDtypeStruct((n_prompt, d), F32),
                   jax.ShapeDtypeStruct((t - n_prompt, d), F32)],
        compiler_params=_params(("arbitrary",)),
        name="moe_combine",
    )(nwin, wstart, wexp, region, cols, y, h, g_final[None, :])


def _owner(cum, q):
    return jnp.sum(cum[None, :] <= q[:, None], axis=1).astype(I32)


def _moe_plan(cin, cnt, n_tok):
    nb = n_tok // ROW_TILE
    n_slots = n_tok * TOP_K
    s_max = n_slots // MOE_TILE + N_EXPERTS
    counts = cnt[:, 0]
    cblk = jnp.concatenate([cin[:, :, 0].T, counts[:, None]], axis=1)
    ntile = (counts + MOE_TILE - 1) // MOE_TILE
    tcum = jnp.cumsum(ntile)
    tstart = tcum - ntile
    total = tcum[-1]
    s_idx = jnp.arange(s_max, dtype=I32)
    tile_ok = s_idx < total
    tile_e = _owner(tcum, jnp.minimum(s_idx, total - 1))
    tile_k = s_idx - tstart[tile_e]
    tile_rows = jnp.where(tile_ok, jnp.clip(counts[tile_e] - tile_k * MOE_TILE, 0, MOE_TILE), 0)
    tile_nsub = ((tile_rows + SUB_ROWS - 1) // SUB_ROWS).astype(I32)
    region = (tstart * MOE_TILE).astype(I32)

    rel = tile_k * MOE_TILE
    cb = cblk[tile_e]
    b_lo = jnp.clip(jnp.sum(cb[:, 1:] <= rel[:, None], axis=1), 0, nb - 1)
    b_hi = jnp.sum(cb[:, :-1] < (rel + tile_rows)[:, None], axis=1) - 1
    b_hi = jnp.clip(jnp.maximum(b_hi, b_lo), 0, nb - 1)
    fplan = (tile_e, tile_nsub, tile_k.astype(I32), b_lo.astype(I32), b_hi.astype(I32),
             cblk.astype(I32))

    span = (cblk[:, 1:] - cblk[:, :-1]).T
    start = (region[:, None] + cblk[:, :-1]).T
    first = (start // BF16_SUBLANES) * BF16_SUBLANES
    nwin = jnp.where(span > 0, (start + span - first + WINDOW_ROWS - 1) // WINDOW_ROWS, 0)
    wcum = jnp.cumsum(nwin, axis=1)
    wfirst = wcum - nwin
    max_windows = N_EXPERTS * (ROW_TILE // WINDOW_ROWS + 1)
    q = jnp.arange(max_windows, dtype=I32)[None, :]
    q = jnp.minimum(q, wcum[:, -1:] - 1)
    wexp = jnp.sum(wcum[:, None, :] <= q[:, :, None], axis=2).astype(I32)
    k = q - jnp.take_along_axis(wfirst, wexp, axis=1)
    wstart = jnp.take_along_axis(first, wexp, axis=1) + k * WINDOW_ROWS
    cplan = (wcum[:, -1].astype(I32), wstart.reshape(-1).astype(I32), wexp.reshape(-1), region)
    return fplan, cplan


def kernel(x_prompt, x_sample, state_pool, g_mix, w_in, g_v, w_pool, pool_scale, w_s, b_s, w_out,
           g_ffn, dense_w_gate, dense_w_up, dense_w_down, w_router, moe_w_gate, moe_w_up,
           moe_w_down, g_final):
    batch, seq, d = x_prompt.shape
    dec_batch, dec_seq, _ = x_sample.shape
    depth = w_in.shape[0]
    n_p = batch * seq
    n_s = dec_batch * dec_seq
    n_tok = n_p + n_s
    assert dec_seq == 8 and depth == 2 and n_p % ROW_TILE == 0 and n_s % ROW_TILE == 0

    addends = ((x_prompt.reshape(n_p, d), x_sample.reshape(n_s, d)),)
    dn = w_in.shape[2] // 3
    hd = dn // N_HEADS
    per_chunk = CHUNK // dec_seq
    pools_p, pools_s, vs_s = [], [], []
    out = None
    w_in_bf16 = w_in.astype(BF16)
    w_out_bf16 = w_out.astype(BF16)
    state_t = jnp.swapaxes(state_pool, 1, 2)
    for i in range(depth):
        p, u, v, v_sample = _inproj(i, addends, g_mix, w_in_bf16, g_v, n_p)
        bias_p = jnp.repeat(b_s[i].T, hd, axis=1)
        ws_s = w_s[i][:, :dec_seq, :dec_seq]
        eye = jnp.eye(per_chunk, dtype=F32)
        wm_s = (eye[None, :, None, :, None] * ws_s[:, None, :, None, :]).reshape(N_HEADS, CHUNK, CHUNK)
        bias_s = jnp.tile(jnp.repeat(b_s[i][:, :dec_seq].T, hd, axis=1), (per_chunk, 1))
        mix_p = _mixer(i, p, u, v, 0, n_p, seq, 0, w_s[i], bias_p, w_pool, pool_scale)
        mix_s = _mixer(i, p, u, v, n_p, n_s, dec_seq, PAST_LEN, wm_s, bias_s, w_pool, pool_scale,
                       state=state_t)
        dense = i % 2 == 0
        h, *xn = _outproj(i, (mix_p, mix_s), w_out_bf16, addends, g_next=g_ffn if dense else None)

        pools_p.append(jnp.stack([p[(b + 1) * seq - POOL_BUF:(b + 1) * seq] for b in range(batch)]))
        pools_s.append(jnp.concatenate(
            [state_pool[i, :, dec_seq:], p[n_p:].reshape(dec_batch, dec_seq, dn)], axis=1))
        vs_s.append(v_sample.reshape(dec_batch, dec_seq, dn))

        j = i // 2
        if dense:
            y = _dense_ffn(j, xn[0], dense_w_gate, dense_w_up, dense_w_down)
            addends = ((h,), (y,))
        else:
            xn, rank, cols, cin, cnt = _router(i, h, g_ffn, w_router[j].T)
            fplan, cplan = _moe_plan(cin, cnt, n_tok)
            y = _moe_ffn(j, xn, rank, fplan, moe_w_gate, moe_w_up, moe_w_down)
            out = _combine(h, y, cols, cplan, g_final, n_p)
            addends = (tuple(out),)

    y_prompt = out[0].reshape(batch, seq, d)
    y_sample = out[1].reshape(dec_batch, dec_seq, d)
    return (y_prompt, y_sample, jnp.stack(pools_p), jnp.stack(pools_s), jnp.stack(vs_s))
```

```python
import functools

import jax
import jax.numpy as jnp
from jax import lax
from jax.experimental import pallas as pl
from jax.experimental.pallas import tpu as pltpu

F32 = jnp.float32
BF16 = jnp.bfloat16
I32 = jnp.int32

EPS = 1e-6
POOL_WINDOWS = (2, 4, 8, 16)
POOL_BUF = max(POOL_WINDOWS) - 1
HALO = 16
SHIFT_PAD = 8
CHUNK = 128
N_HEADS = 8
N_EXPERTS = 8
TOP_K = 2
PAST_LEN = 16384

V7X_VMEM_LIMIT = 56 * 1024 * 1024
V7X_MXU_COLS = 256

ROW_TILE = 512
FF_TILE = 256
SUB_ROWS = 256
BLOCK_SUBS = 5
MOE_TILE = 2560
WINDOW_ROWS = 256
BF16_SUBLANES = 16
DMA_SLOTS = 4
DENSE_TILE = 2304


def _params(sem):
    return pltpu.CompilerParams(dimension_semantics=sem, vmem_limit_bytes=V7X_VMEM_LIMIT)


def _rms(x, g):
    return x * lax.rsqrt(jnp.mean(x * x, axis=-1, keepdims=True) + EPS) * g


def _gelu(x):
    return 0.5 * x * (1.0 + lax.erf(x * (2.0 ** -0.5)))


def _stack_offsets(stack):
    offs, o = [], 0
    for a in stack:
        offs.append(o)
        o += a.shape[0] // ROW_TILE
    return tuple(offs), o


def _stack_specs(stack, cols, m_axis, col_fn=None):
    offs, _ = _stack_offsets(stack)
    specs = []
    for a, off in zip(stack, offs):
        nblk = a.shape[0] // ROW_TILE

        def imap(*idx, off=off, nblk=nblk):
            col = 0 if col_fn is None else col_fn(*idx)
            return (jnp.clip(idx[m_axis] - off, 0, nblk - 1), col)

        specs.append(pl.BlockSpec((ROW_TILE, cols), imap))
    return specs


def _stack_read(refs, offs, m):
    v = refs[0][...]
    for ref, off in zip(refs[1:], offs[1:]):
        v = jnp.where(m >= off, ref[...], v)
    return v


def _sum_read(refs, layout, m):
    total, k = None, 0
    for offs in layout:
        part = _stack_read(refs[k:k + len(offs)], offs, m)
        k += len(offs)
        total = part if total is None else total + part
    return total


def _inproj_body(*refs, layout, sample_block0):
    n_x = sum(len(offs) for offs in layout)
    g_ref, w_ref, gv_ref, p_ref, u_ref, v_ref, vs_ref, vtmp_ref = refs[n_x:]
    m = pl.program_id(0)
    x = _sum_read(refs[:n_x], layout, m)
    rowscale = lax.rsqrt(jnp.mean(x * x, axis=-1, keepdims=True) + EPS)
    xg = (x * g_ref[...]).astype(BF16)
    dn = p_ref.shape[1]
    ss = None
    for k, o_ref in enumerate((p_ref, u_ref, vtmp_ref)):
        for c0 in range(0, dn, V7X_MXU_COLS):
            w = w_ref[:, k * dn + c0:k * dn + c0 + V7X_MXU_COLS]
            acc = jnp.dot(xg, w, preferred_element_type=F32) * rowscale
            if o_ref is not p_ref:
                acc = _gelu(acc)
            if o_ref is vtmp_ref:
                part = jnp.sum(acc * acc, axis=1, keepdims=True)
                ss = part if ss is None else ss + part
            o_ref[:, c0:c0 + V7X_MXU_COLS] = acc.astype(o_ref.dtype)
    vn = vtmp_ref[...] * lax.rsqrt(ss / dn + EPS) * gv_ref[...]
    v_ref[...] = vn.astype(BF16)

    @pl.when(m >= sample_block0)
    def _():
        vs_ref[...] = vn


def _inproj(layer, addends, g_mix, w_in_bf16, g_v, n_prompt):
    d = addends[0][0].shape[1]
    dn = w_in_bf16.shape[2] // 3
    layout = tuple(_stack_offsets(st)[0] for st in addends)
    n_blk = _stack_offsets(addends[0])[1]
    x_specs = [sp for st in addends for sp in _stack_specs(st, d, 0)]
    out_spec = pl.BlockSpec((ROW_TILE, dn), lambda m: (m, 0))
    n_tok = n_blk * ROW_TILE
    npb = n_prompt // ROW_TILE
    return pl.pallas_call(
        functools.partial(_inproj_body, layout=layout, sample_block0=npb),
        grid=(n_blk,),
        in_specs=x_specs + [
            pl.BlockSpec((None, 1, d), lambda m: (layer, 0, 0)),
            pl.BlockSpec((None, d, 3 * dn), lambda m: (layer, 0, 0), pipeline_mode=pl.Buffered(1)),
            pl.BlockSpec((None, 1, dn), lambda m: (layer, 0, 0)),
        ],
        out_specs=[out_spec] * 3 + [
            pl.BlockSpec((ROW_TILE, dn), lambda m: (jnp.maximum(m - npb, 0), 0))],
        out_shape=[jax.ShapeDtypeStruct((n_tok, dn), F32),
                   jax.ShapeDtypeStruct((n_tok, dn), BF16),
                   jax.ShapeDtypeStruct((n_tok, dn), BF16),
                   jax.ShapeDtypeStruct((n_tok - n_prompt, dn), F32)],
        scratch_shapes=[pltpu.VMEM((ROW_TILE, dn), F32)],
        compiler_params=_params(("arbitrary",)),
        name=f"inproj{layer}",
    )(*[a for st in addends for a in st], g_mix[:, None, :], w_in_bf16, g_v[:, None, :])


def _mixer_body(*refs, rows, blocks_per_seq, start_pos, sample):
    p_ref, hist_ref, u_ref, v_ref, wm_ref, bias_ref, wp_ref, ps_ref, o_ref, full_ref = refs[:10]
    d_pool = p_ref.shape[1]
    gd = d_pool // len(POOL_WINDOWS)
    hd = u_ref.shape[1] // N_HEADS
    blk = pl.program_id(0) % blocks_per_seq
    p = p_ref[...]

    if sample:
        seqs = rows // 8
        for k in range(POOL_BUF):
            full_ref[:, 1 + k, :] = hist_ref[k]
        full_ref[:, HALO:HALO + 8, :] = p.reshape(seqs, 8, d_pool)
    else:
        level_refs = refs[10:]
        ext = HALO + rows
        zeros = jnp.zeros((SHIFT_PAD, d_pool), F32)
        full_ref[0:SHIFT_PAD, :] = zeros
        full_ref[SHIFT_PAD:SHIFT_PAD + HALO, :] = jnp.where(blk == 0, 0.0, hist_ref[...])
        full_ref[SHIFT_PAD + HALO:SHIFT_PAD + ext, :] = p
        for ref in level_refs:
            ref[0:SHIFT_PAD, :] = zeros[:, :gd]
    row_id = lax.broadcasted_iota(I32, (rows, 1), 0)
    pos = start_pos + (row_id % 8 if sample else blk * rows + row_id)
    for gi, w in enumerate(POOL_WINDOWS):
        c0 = gi * gd
        if sample:
            s = None
            for j in range(w):
                term = full_ref[:, HALO - j:HALO - j + 8, c0:c0 + gd].reshape(rows, gd)
                s = term if s is None else s + term
        else:
            def read(lo, hi, c0=c0):
                return full_ref[lo:hi, c0:c0 + gd]

            shift, level = 1, 0
            while shift < w:
                both = (read(SHIFT_PAD, SHIFT_PAD + ext)
                        + read(SHIFT_PAD - shift, SHIFT_PAD + ext - shift))
                shift *= 2
                if shift < w:
                    dst = level_refs[level % 2]
                    dst[SHIFT_PAD:SHIFT_PAD + ext, :] = both

                    def read(lo, hi, dst=dst):
                        return dst[lo:hi, :]

                    level += 1
            s = both[HALO:, :]
        cnt = jnp.minimum(pos + 1, w).astype(F32)
        r = s / cnt - p[:, c0:c0 + gd]
        y = jnp.dot(r.astype(BF16), wp_ref[gi].astype(BF16), preferred_element_type=F32)
        o_ref[:, c0:c0 + gd] = (y * ps_ref[:, c0:c0 + gd]).astype(BF16)

    row = lax.broadcasted_iota(I32, (CHUNK, CHUNK), 0)
    col = lax.broadcasted_iota(I32, (CHUNK, CHUNK), 1)
    for h in range(N_HEADS):
        c0 = h * hd
        wm = jnp.where(row >= col, wm_ref[h], 0.0).astype(BF16)
        for c in range(rows // CHUNK):
            r0 = c * CHUNK
            z = jnp.dot(wm, v_ref[r0:r0 + CHUNK, c0:c0 + hd].astype(BF16),
                        preferred_element_type=F32) + bias_ref[:, c0:c0 + hd]
            o_ref[r0:r0 + CHUNK, d_pool + c0:d_pool + c0 + hd] = (
                u_ref[r0:r0 + CHUNK, c0:c0 + hd] * z).astype(BF16)


def _mixer(layer, p, u, v, row0, n_rows, seq_len, start_pos, wm, bias, w_pool, pool_scale,
           state=None):
    dn = p.shape[1]
    sample = state is not None
    rows = ROW_TILE
    blocks_per_seq = 1 if sample else seq_len // rows
    b0 = row0 // rows
    blk_spec = pl.BlockSpec((rows, dn), lambda i: (b0 + i, 0))
    if sample:
        hist_spec = pl.BlockSpec((None, POOL_BUF, rows // 8, dn), lambda i: (layer, 0, i, 0))
        hist = state
        scratch = [pltpu.VMEM((rows // 8, HALO + 8, dn), F32)]
    else:
        per = rows // HALO
        hist_spec = pl.BlockSpec((HALO, dn), lambda i: (jnp.maximum((b0 + i) * per - 1, 0), 0))
        hist = p
        ext = SHIFT_PAD + HALO + rows
        gd = dn // len(POOL_WINDOWS)
        scratch = [pltpu.VMEM((ext, dn), F32), pltpu.VMEM((ext, gd), F32),
                   pltpu.VMEM((ext, gd), F32)]
    return pl.pallas_call(
        functools.partial(_mixer_body, rows=rows, blocks_per_seq=blocks_per_seq,
                          start_pos=start_pos, sample=sample),
        grid=(n_rows // rows,),
        in_specs=[
            blk_spec, hist_spec, blk_spec, blk_spec,
            pl.BlockSpec((N_HEADS, CHUNK, CHUNK), lambda i: (0, 0, 0)),
            pl.BlockSpec((CHUNK, dn), lambda i: (0, 0)),
            pl.BlockSpec((None,) + w_pool.shape[1:], lambda i: (layer, 0, 0, 0)),
            pl.BlockSpec((None, 1, dn), lambda i: (layer, 0, 0)),
        ],
        out_specs=pl.BlockSpec((rows, 2 * dn), lambda i: (i, 0)),
        out_shape=jax.ShapeDtypeStruct((n_rows, 2 * dn), BF16),
        scratch_shapes=scratch,
        compiler_params=_params(("arbitrary",)),
        name=f"mixer{layer}_{'sample' if sample else 'prompt'}",
    )(p, hist, u, v, wm, bias, w_pool, pool_scale[:, None, :])


def _outproj_body(*refs, mix_offs, layout, with_norm):
    n_mix = len(mix_offs)
    n_h = sum(len(offs) for offs in layout)
    w_ref = refs[n_mix]
    h_refs = refs[n_mix + 1:n_mix + 1 + n_h]
    if with_norm:
        g_ref, o_ref, xn_ref = refs[n_mix + 1 + n_h:]
    else:
        o_ref, = refs[n_mix + 1 + n_h:]
    m = pl.program_id(0)
    a = _stack_read(refs[:n_mix], mix_offs, m)
    h = _sum_read(h_refs, layout, m)
    d = o_ref.shape[1]
    ss = None
    for c0 in range(0, d, V7X_MXU_COLS):
        cols = slice(c0, c0 + V7X_MXU_COLS)
        hn = h[:, cols] + jnp.dot(a, w_ref[:, cols], preferred_element_type=F32)
        o_ref[:, cols] = hn
        if with_norm:
            part = jnp.sum(hn * hn, axis=1, keepdims=True)
            ss = part if ss is None else ss + part
    if with_norm:
        xn_ref[...] = (o_ref[...] * lax.rsqrt(ss / d + EPS) * g_ref[...]).astype(BF16)


def _outproj(layer, mix, w_out_bf16, addends, g_next=None):
    k = mix[0].shape[1]
    d = w_out_bf16.shape[2]
    mix_offs, n_blk = _stack_offsets(mix)
    layout = tuple(_stack_offsets(st)[0] for st in addends)
    with_norm = g_next is not None
    row_spec = pl.BlockSpec((ROW_TILE, d), lambda m: (m, 0))
    in_specs = _stack_specs(mix, k, 0) + [
        pl.BlockSpec((None, k, d), lambda m: (layer, 0, 0), pipeline_mode=pl.Buffered(1)),
    ] + [sp for st in addends for sp in _stack_specs(st, d, 0)]
    operands = [*mix, w_out_bf16, *[a for st in addends for a in st]]
    out_specs = [row_spec]
    out_shape = [jax.ShapeDtypeStruct((n_blk * ROW_TILE, d), F32)]
    if with_norm:
        in_specs.append(pl.BlockSpec((None, 1, d), lambda m: (layer, 0, 0)))
        operands.append(g_next[:, None, :])
        out_specs.append(row_spec)
        out_shape.append(jax.ShapeDtypeStruct((n_blk * ROW_TILE, d), BF16))
    return pl.pallas_call(
        functools.partial(_outproj_body, mix_offs=mix_offs, layout=layout, with_norm=with_norm),
        grid=(n_blk,),
        in_specs=in_specs,
        out_specs=out_specs,
        out_shape=out_shape,
        compiler_params=_params(("arbitrary",)),
        name=f"outproj{layer}",
    )(*operands)


def _split_bf16(x, pieces):
    out = []
    for _ in range(pieces):
        term = x.astype(BF16)
        out.append(term)
        x = x - term.astype(F32)
    return out


def _dot_nt(a, b):
    return lax.dot_general(a, b, (((1,), (1,)), ((), ())), preferred_element_type=F32)


def _router_body(h_ref, g_ref, wr_ref, xn_ref, rank_ref, col_ref, cin_ref, cnt_ref,
                 carry_ref, upper_ref, ident_ref):
    tb = h_ref.shape[0]

    @pl.when(pl.program_id(0) == 0)
    def _():
        carry_ref[...] = jnp.zeros_like(carry_ref)
        row = lax.broadcasted_iota(I32, (tb, tb), 0)
        col = lax.broadcasted_iota(I32, (tb, tb), 1)
        upper_ref[...] = jnp.where(row < col, 1.0, 0.0).astype(BF16)
        ident_ref[...] = jnp.where(row == col, 1.0, 0.0).astype(BF16)

    xn = _rms(h_ref[...], g_ref[...])
    x_hi, x_lo = _split_bf16(xn, 2)
    xn_ref[...] = x_hi
    w_hi, w_lo = _split_bf16(wr_ref[...], 2)
    logits = _dot_nt(w_hi, x_hi) + (_dot_nt(w_hi, x_lo) + _dot_nt(w_lo, x_hi))
    eidx = lax.broadcasted_iota(I32, logits.shape, 0)
    m1 = jnp.max(logits, axis=0, keepdims=True)
    i1 = jnp.min(jnp.where(logits == m1, eidx, N_EXPERTS), axis=0, keepdims=True)
    sel1 = eidx == i1
    rest = jnp.where(sel1, -jnp.inf, logits)
    m2 = jnp.max(rest, axis=0, keepdims=True)
    i2 = jnp.min(jnp.where(rest == m2, eidx, N_EXPERTS), axis=0, keepdims=True)
    sel2 = eidx == i2
    e2 = jnp.exp(m2 - m1)
    g1 = 1.0 / (1.0 + e2)
    g2 = e2 / (1.0 + e2)
    gate = jnp.where(sel1, g1, jnp.where(sel2, g2, 0.0))
    routed = jnp.where(sel1, 1.0, jnp.where(sel2, 1.0, 0.0))
    excl = jnp.dot(routed.astype(BF16), upper_ref[...], preferred_element_type=F32)
    carry = carry_ref[...]
    rank = jnp.where(routed > 0.0, excl + carry[:, 0:1], -1.0)
    rank_ref[...] = rank.astype(I32)
    ident = ident_ref[...]
    d_hi, d_mid, d_lo = _split_bf16(jnp.concatenate([rank, gate], axis=0), 3)
    col_ref[...] = (_dot_nt(ident, d_hi) + _dot_nt(ident, d_mid)) + _dot_nt(ident, d_lo)
    cin_ref[...] = carry.astype(I32)
    new_carry = carry + jnp.sum(routed, axis=1, keepdims=True)
    carry_ref[...] = new_carry
    cnt_ref[...] = new_carry.astype(I32)


def _router(layer, h, g, w_router_t):
    t, d = h.shape
    nb = t // ROW_TILE
    e = w_router_t.shape[0]
    return pl.pallas_call(
        _router_body,
        grid=(nb,),
        in_specs=[pl.BlockSpec((ROW_TILE, d), lambda i: (i, 0)),
                  pl.BlockSpec((None, 1, d), lambda i: (layer, 0, 0)),
                  pl.BlockSpec((e, d), lambda i: (0, 0))],
        out_specs=[pl.BlockSpec((ROW_TILE, d), lambda i: (i, 0)),
                   pl.BlockSpec((e, ROW_TILE), lambda i: (0, i)),
                   pl.BlockSpec((ROW_TILE, 2 * e), lambda i: (i, 0)),
                   pl.BlockSpec((None, e, 128), lambda i: (i, 0, 0)),
                   pl.BlockSpec((e, 128), lambda i: (0, 0))],
        out_shape=[jax.ShapeDtypeStruct((t, d), BF16),
                   jax.ShapeDtypeStruct((e, t), I32),
                   jax.ShapeDtypeStruct((t, 2 * e), F32),
                   jax.ShapeDtypeStruct((nb, e, 128), I32),
                   jax.ShapeDtypeStruct((e, 128), I32)],
        scratch_shapes=[pltpu.VMEM((e, 128), F32), pltpu.VMEM((ROW_TILE, ROW_TILE), BF16),
                        pltpu.VMEM((ROW_TILE, ROW_TILE), BF16)],
        compiler_params=_params(("arbitrary",)),
        name=f"router{layer}",
    )(h, g[:, None, :], w_router_t)


def _swiglu_blocks(x_ref, wg_ref, wu_ref, wd_ref, acc_ref, nsub):
    def rows_block(first_sub, n_sub):
        n_rows = n_sub * SUB_ROWS
        r0 = first_sub * SUB_ROWS
        if not isinstance(r0, int):
            r0 = pl.multiple_of(r0, SUB_ROWS)
        x = x_ref[pl.ds(r0, n_rows), :]
        g = jnp.dot(x, wg_ref[...].astype(BF16), preferred_element_type=F32)
        u = jnp.dot(x, wu_ref[...].astype(BF16), preferred_element_type=F32)
        a = (g * jax.nn.sigmoid(g) * u).astype(BF16)
        acc_ref[pl.ds(r0, n_rows), :] += jnp.dot(a, wd_ref[...].astype(BF16),
                                                 preferred_element_type=F32)

    big = BLOCK_SUBS
    if isinstance(nsub, int):
        n_blocks = -(-nsub // big)
        first = 0
        for m in range(n_blocks):
            size = nsub // n_blocks + (1 if m < nsub % n_blocks else 0)
            rows_block(first, size)
            first += size
        return

    def full_block(m, carry):
        rows_block(m * big, big)
        return carry

    lax.fori_loop(0, nsub // big, full_block, 0)

    for rest in range(1, big):
        @pl.when(nsub % big == rest)
        def _():
            rows_block((nsub // big) * big, rest)


def _dense_ffn_body(x_ref, wg_ref, wu_ref, wd_ref, o_ref):
    @pl.when(pl.program_id(1) == 0)
    def _():
        o_ref[...] = jnp.zeros_like(o_ref)

    _swiglu_blocks(x_ref, wg_ref, wu_ref, wd_ref, o_ref, x_ref.shape[0] // SUB_ROWS)


def _dense_ffn(layer, x, wg, wu, wd):
    rows, d = x.shape
    nf = wg.shape[2] // FF_TILE
    return pl.pallas_call(
        _dense_ffn_body,
        grid=(rows // DENSE_TILE, nf),
        in_specs=[pl.BlockSpec((DENSE_TILE, d), lambda s, f: (s, 0), pipeline_mode=pl.Buffered(1)),
                  pl.BlockSpec((None, d, FF_TILE), lambda s, f: (layer, 0, f)),
                  pl.BlockSpec((None, d, FF_TILE), lambda s, f: (layer, 0, f)),
                  pl.BlockSpec((None, FF_TILE, d), lambda s, f: (layer, f, 0))],
        out_specs=pl.BlockSpec((DENSE_TILE, d), lambda s, f: (s, 0), pipeline_mode=pl.Buffered(1)),
        out_shape=jax.ShapeDtypeStruct((rows, d), F32),
        compiler_params=_params(("arbitrary", "arbitrary")),
        name=f"ffn_dense{layer}",
    )(x, wg, wu, wd)


def _moe_ffn_body(te_ref, ns_ref, tk_ref, blo_ref, bhi_ref, cb_ref,
                  rank_ref, xn_hbm, wg_ref, wu_ref, wd_ref, y_hbm,
                  xs_ref, acc_ref, xbuf_ref, in_sem, out_sem, *, n_blocks):
    s = pl.program_id(0)
    f = pl.program_id(1)
    nsub = ns_ref[s]
    e = te_ref[s]
    tile_rows = xs_ref.shape[0]
    tb = xbuf_ref.shape[1]
    rel0 = tk_ref[s] * tile_rows

    def fetch(b, slot):
        src = xn_hbm.at[pl.ds(pl.multiple_of(b * tb, tb), tb), :]
        return pltpu.make_async_copy(src, xbuf_ref.at[slot], in_sem.at[slot])

    @pl.when((f == 0) & (nsub > 0))
    def _():
        blo = blo_ref[s]
        bhi = bhi_ref[s]
        for ahead in range(DMA_SLOTS - 1):
            @pl.when(blo + ahead <= bhi)
            def _():
                fetch(blo + ahead, ahead).start()

        acc_ref[...] = jnp.zeros_like(acc_ref)
        xs_ref[...] = jnp.zeros_like(xs_ref)

        def token_block(b, carry):
            slot = (b - blo) % DMA_SLOTS
            fetch(b, slot).wait()
            nxt = b + (DMA_SLOTS - 1)

            @pl.when(nxt <= bhi)
            def _():
                fetch(nxt, (nxt - blo) % DMA_SLOTS).start()

            c0 = cb_ref[e * (n_blocks + 1) + b]
            c1 = cb_ref[e * (n_blocks + 1) + b + 1]
            j_lo = jnp.clip((c0 - rel0) // SUB_ROWS, 0, nsub - 1)
            j_hi = jnp.clip((c1 - 1 - rel0) // SUB_ROWS, 0, nsub - 1)
            rank_row = rank_ref[pl.ds(e, 1), pl.ds(pl.multiple_of(b * tb, tb), tb)]
            xb = xbuf_ref.at[slot]

            def sub_block(j, c):
                r0 = pl.multiple_of(j * SUB_ROWS, SUB_ROWS)
                want = lax.broadcasted_iota(I32, (SUB_ROWS, tb), 0) + (rel0 + r0)
                onehot = jnp.where(want == rank_row, 1.0, 0.0).astype(BF16)
                xs_ref[pl.ds(r0, SUB_ROWS), :] += jnp.dot(
                    onehot, xb[...], preferred_element_type=F32).astype(BF16)
                return c

            lax.fori_loop(j_lo, j_hi + 1, sub_block, 0)
            return carry

        lax.fori_loop(blo, bhi + 1, token_block, 0)

    _swiglu_blocks(xs_ref, wg_ref, wu_ref, wd_ref, acc_ref, nsub)

    @pl.when(f == pl.num_programs(1) - 1)
    def _():
        @pl.when(nsub > 0)
        def _():
            xs_ref[...] = acc_ref[...].astype(BF16)

        @pl.when(nsub == 0)
        def _():
            xs_ref[...] = jnp.zeros_like(xs_ref)

        dst = y_hbm.at[pl.ds(pl.multiple_of(s * tile_rows, tile_rows), tile_rows), :]
        out = pltpu.make_async_copy(xs_ref, dst, out_sem.at[0])
        out.start()
        out.wait()

        @pl.when(s == pl.num_programs(0) - 1)
        def _():
            xs_ref[0:WINDOW_ROWS, :] = jnp.zeros((WINDOW_ROWS, xs_ref.shape[1]), BF16)
            end = y_hbm.shape[0] - WINDOW_ROWS
            tail = pltpu.make_async_copy(xs_ref.at[pl.ds(0, WINDOW_ROWS), :],
                                         y_hbm.at[pl.ds(end, WINDOW_ROWS), :], out_sem.at[0])
            tail.start()
            tail.wait()


def _moe_ffn(layer, xn, rank, plan, wg, wu, wd):
    tile_e, tile_nsub, tile_k, tile_blo, tile_bhi, cblk = plan
    t, d = xn.shape
    ff = wg.shape[3]
    nf = ff // FF_TILE
    n_tiles = tile_e.shape[0]
    nb = t // ROW_TILE

    def w_col(s, f, te, ns, *_):
        return (layer, te[s], 0, jnp.where(ns[s] > 0, f, nf - 1))

    def w_row(s, f, te, ns, *_):
        return (layer, te[s], jnp.where(ns[s] > 0, f, nf - 1), 0)

    return pl.pallas_call(
        functools.partial(_moe_ffn_body, n_blocks=nb),
        grid_spec=pltpu.PrefetchScalarGridSpec(
            num_scalar_prefetch=6,
            grid=(n_tiles, nf),
            in_specs=[pl.BlockSpec(rank.shape, lambda s, f, *_: (0, 0), pipeline_mode=pl.Buffered(1)),
                      pl.BlockSpec(memory_space=pl.ANY),
                      pl.BlockSpec((None, None, d, FF_TILE), w_col),
                      pl.BlockSpec((None, None, d, FF_TILE), w_col),
                      pl.BlockSpec((None, None, FF_TILE, d), w_row)],
            out_specs=pl.BlockSpec(memory_space=pl.ANY),
            scratch_shapes=[pltpu.VMEM((MOE_TILE, d), BF16),
                            pltpu.VMEM((MOE_TILE, d), F32),
                            pltpu.VMEM((DMA_SLOTS, ROW_TILE, d), BF16),
                            pltpu.SemaphoreType.DMA((DMA_SLOTS,)),
                            pltpu.SemaphoreType.DMA((1,))]),
        out_shape=jax.ShapeDtypeStruct((n_tiles * MOE_TILE + WINDOW_ROWS, d), BF16),
        compiler_params=_params(("arbitrary", "arbitrary")),
        name=f"ffn_moe{layer}",
    )(tile_e, tile_nsub, tile_k, tile_blo, tile_bhi, cblk.reshape(-1), rank, xn, wg, wu, wd)


def _combine_body(nwin_ref, ws_ref, we_ref, region_ref, col_ref, y_hbm, h_ref, gf_ref,
                  op_ref, os_ref, acc_ref, ybuf_ref, sem, *, prompt_blocks, max_windows):
    b = pl.program_id(0)
    tb = h_ref.shape[0]
    n = nwin_ref[b]
    base = b * max_windows

    def fetch(i, slot, block=b):
        start = pl.multiple_of(ws_ref[block * max_windows + i], BF16_SUBLANES)
        return pltpu.make_async_copy(y_hbm.at[pl.ds(start, WINDOW_ROWS), :], ybuf_ref.at[slot],
                                     sem.at[slot])

    def start_first_windows(block):
        for ahead in range(DMA_SLOTS - 1):
            @pl.when(ahead < nwin_ref[block])
            def _():
                fetch(ahead, ahead, block).start()

    @pl.when(b == 0)
    def _():
        start_first_windows(b)

    acc_ref[...] = h_ref[...]

    def window(i, carry):
        slot = i % DMA_SLOTS
        fetch(i, slot).wait()
        nxt = i + (DMA_SLOTS - 1)

        @pl.when(nxt < n)
        def _():
            fetch(nxt, nxt % DMA_SLOTS).start()

        e = we_ref[base + i]
        cols = col_ref[...]
        lane = lax.broadcasted_iota(I32, cols.shape, 1)
        rank_col = jnp.sum(jnp.where(lane == e, cols, 0.0), axis=1, keepdims=True)
        gate_col = jnp.sum(jnp.where(lane == e + N_EXPERTS, cols, 0.0), axis=1, keepdims=True)
        first_rank = ws_ref[base + i] - region_ref[e]
        want = (lax.broadcasted_iota(I32, (tb, WINDOW_ROWS), 1) + first_rank).astype(F32)
        onehot = jnp.where(want == rank_col, 1.0, 0.0).astype(BF16)
        picked = jnp.dot(onehot, ybuf_ref[slot], preferred_element_type=F32)
        acc_ref[...] += gate_col * picked
        return carry

    lax.fori_loop(0, n, window, 0)

    @pl.when(b + 1 < pl.num_programs(0))
    def _():
        start_first_windows(b + 1)

    res = _rms(acc_ref[...], gf_ref[...])

    @pl.when(b < prompt_blocks)
    def _():
        op_ref[...] = res

    @pl.when(b >= prompt_blocks)
    def _():
        os_ref[...] = res


def _combine(h, y, cols, plan, g_final, n_prompt):
    t, d = h.shape
    nwin, wstart, wexp, region = plan
    nb = t // ROW_TILE
    npb = n_prompt // ROW_TILE
    return pl.pallas_call(
        functools.partial(_combine_body, prompt_blocks=npb, max_windows=wstart.shape[0] // nb),
        grid_spec=pltpu.PrefetchScalarGridSpec(
            num_scalar_prefetch=4,
            grid=(nb,),
            in_specs=[pl.BlockSpec((ROW_TILE, cols.shape[1]), lambda b, *_: (b, 0)),
                      pl.BlockSpec(memory_space=pl.ANY),
                      pl.BlockSpec((ROW_TILE, d), lambda b, *_: (b, 0)),
                      pl.BlockSpec((1, d), lambda b, *_: (0, 0))],
            out_specs=[pl.BlockSpec((ROW_TILE, d), lambda b, *_: (jnp.minimum(b, npb - 1), 0)),
                       pl.BlockSpec((ROW_TILE, d), lambda b, *_: (jnp.maximum(b - npb, 0), 0))],
            scratch_shapes=[pltpu.VMEM((ROW_TILE, d), F32),
                            pltpu.VMEM((DMA_SLOTS, WINDOW_ROWS, d), BF16),
                            pltpu.SemaphoreType.DMA((DMA_SLOTS,))]),
        out_shape=[jax.ShapeDtypeStruct((n_prompt, d), F32),
                   jax.ShapeDtypeStruct((t - n_prompt, d), F32)],
        compiler_params=_params(("arbitrary",)),
        name="moe_combine",
    )(nwin, wstart, wexp, region, cols, y, h, g_final[None, :])


def _owner(cum, q):
    return jnp.sum(cum[None, :] <= q[:, None], axis=1).astype(I32)


def _moe_plan(cin, cnt, n_tok):
    nb = n_tok // ROW_TILE
    n_slots = n_tok * TOP_K
    s_max = n_slots // MOE_TILE + N_EXPERTS
    counts = cnt[:, 0]
    cblk = jnp.concatenate([cin[:, :, 0].T, counts[:, None]], axis=1)
    ntile = (counts + MOE_TILE - 1) // MOE_TILE
    tcum = jnp.cumsum(ntile)
    tstart = tcum - ntile
    total = tcum[-1]
    s_idx = jnp.arange(s_max, dtype=I32)
    tile_ok = s_idx < total
    tile_e = _owner(tcum, jnp.minimum(s_idx, total - 1))
    tile_k = s_idx - tstart[tile_e]
    tile_rows = jnp.where(tile_ok, jnp.clip(counts[tile_e] - tile_k * MOE_TILE, 0, MOE_TILE), 0)
    tile_nsub = ((tile_rows + SUB_ROWS - 1) // SUB_ROWS).astype(I32)
    region = (tstart * MOE_TILE).astype(I32)

    rel = tile_k * MOE_TILE
    cb = cblk[tile_e]
    b_lo = jnp.clip(jnp.sum(cb[:, 1:] <= rel[:, None], axis=1), 0, nb - 1)
    b_hi = jnp.sum(cb[:, :-1] < (rel + tile_rows)[:, None], axis=1) - 1
    b_hi = jnp.clip(jnp.maximum(b_hi, b_lo), 0, nb - 1)
    fplan = (tile_e, tile_nsub, tile_k.astype(I32), b_lo.astype(I32), b_hi.astype(I32),
             cblk.astype(I32))

    span = (cblk[:, 1:] - cblk[:, :-1]).T
    start = (region[:, None] + cblk[:, :-1]).T
    first = (start // BF16_SUBLANES) * BF16_SUBLANES
    nwin = jnp.where(span > 0, (start + span - first + WINDOW_ROWS - 1) // WINDOW_ROWS, 0)
    wcum = jnp.cumsum(nwin, axis=1)
    wfirst = wcum - nwin
    max_windows = N_EXPERTS * (ROW_TILE // WINDOW_ROWS + 1)
    q = jnp.arange(max_windows, dtype=I32)[None, :]
    q = jnp.minimum(q, wcum[:, -1:] - 1)
    wexp = jnp.sum(wcum[:, None, :] <= q[:, :, None], axis=2).astype(I32)
    k = q - jnp.take_along_axis(wfirst, wexp, axis=1)
    wstart = jnp.take_along_axis(first, wexp, axis=1) + k * WINDOW_ROWS
    cplan = (wcum[:, -1].astype(I32), wstart.reshape(-1).astype(I32), wexp.reshape(-1), region)
    return fplan, cplan


def kernel(x_prompt, x_sample, state_pool, g_mix, w_in, g_v, w_pool, pool_scale, w_s, b_s, w_out,
           g_ffn, dense_w_gate, dense_w_up, dense_w_down, w_router, moe_w_gate, moe_w_up,
           moe_w_down, g_final):
    batch, seq, d = x_prompt.shape
    dec_batch, dec_seq, _ = x_sample.shape
    depth = w_in.shape[0]
    n_p = batch * seq
    n_s = dec_batch * dec_seq
    n_tok = n_p + n_s
    assert dec_seq == 8 and depth == 2 and n_p % ROW_TILE == 0 and n_s % ROW_TILE == 0

    addends = ((x_prompt.reshape(n_p, d), x_sample.reshape(n_s, d)),)
    dn = w_in.shape[2] // 3
    hd = dn // N_HEADS
    per_chunk = CHUNK // dec_seq
    pools_p, pools_s, vs_s = [], [], []
    out = None
    w_in_bf16 = w_in.astype(BF16)
    w_out_bf16 = w_out.astype(BF16)
    state_t = jnp.swapaxes(state_pool, 1, 2)
    for i in range(depth):
        p, u, v, v_sample = _inproj(i, addends, g_mix, w_in_bf16, g_v, n_p)
        bias_p = jnp.repeat(b_s[i].T, hd, axis=1)
        ws_s = w_s[i][:, :dec_seq, :dec_seq]
        eye = jnp.eye(per_chunk, dtype=F32)
        wm_s = (eye[None, :, None, :, None] * ws_s[:, None, :, None, :]).reshape(N_HEADS, CHUNK, CHUNK)
        bias_s = jnp.tile(jnp.repeat(b_s[i][:, :dec_seq].T, hd, axis=1), (per_chunk, 1))
        mix_p = _mixer(i, p, u, v, 0, n_p, seq, 0, w_s[i], bias_p, w_pool, pool_scale)
        mix_s = _mixer(i, p, u, v, n_p, n_s, dec_seq, PAST_LEN, wm_s, bias_s, w_pool, pool_scale,
                       state=state_t)
        dense = i % 2 == 0
        h, *xn = _outproj(i, (mix_p, mix_s), w_out_bf16, addends, g_next=g_ffn if dense else None)

        pools_p.append(jnp.stack([p[(b + 1) * seq - POOL_BUF:(b + 1) * seq] for b in range(batch)]))
        pools_s.append(jnp.concatenate(
            [state_pool[i, :, dec_seq:], p[n_p:].reshape(dec_batch, dec_seq, dn)], axis=1))
        vs_s.append(v_sample.reshape(dec_batch, dec_seq, dn))

        j = i // 2
        if dense:
            y = _dense_ffn(j, xn[0], dense_w_gate, dense_w_up, dense_w_down)
            addends = ((h,), (y,))
        else:
            xn, rank, cols, cin, cnt = _router(i, h, g_ffn, w_router[j].T)
            fplan, cplan = _moe_plan(cin, cnt, n_tok)
            y = _moe_ffn(j, xn, rank, fplan, moe_w_gate, moe_w_up, moe_w_down)
            out = _combine(h, y, cols, cplan, g_final, n_p)
            addends = (tuple(out),)

    y_prompt = out[0].reshape(batch, seq, d)
    y_sample = out[1].reshape(dec_batch, dec_seq, d)
    return (y_prompt, y_sample, jnp.stack(pools_p), jnp.stack(pools_s), jnp.stack(vs_s))
```
